```python
import math
import jax
import jax.numpy as jnp
from jax import lax
import numpy as np

D_MODEL = 2048
BATCH = 8
SEQ = 2048
DEPTH = 4

N_EVEN = (DEPTH + 1) // 2
N_ODD = DEPTH // 2
D_FF = ((8 * D_MODEL // 3 + 255) // 256) * 256
MIX_WIDTH = D_MODEL
POOL_WIDTH = MIX_WIDTH // 2
POOL_WINDOWS = (2, 4, 8, 16)
POOL_GROUPS = len(POOL_WINDOWS)
POOL_GROUP_WIDTH = POOL_WIDTH // POOL_GROUPS
DIFF_HEAD_DIM = 128
DIFF_V_DIM = 2 * DIFF_HEAD_DIM
ATTN_WIDTH = MIX_WIDTH - POOL_WIDTH
N_HEADS_DIFF = ATTN_WIDTH // DIFF_V_DIM
AB_IN_WIDTH = POOL_WIDTH + 3 * ATTN_WIDTH
CONV_WIDTH = 3
REL_BUCKETS = 32
REL_MAX_DIST = 128
Q_BLOCK = 128
NORM_EPS = 1e-6
NEG_INF = -1e30

kernel_name = "hybrid_pool_diffattn_shortconv_macaron"


def _rmsnorm(x, g):
    xf = x.astype(jnp.float32)
    xf = xf * lax.rsqrt(jnp.mean(xf * xf, axis=-1, keepdims=True) + NORM_EPS)
    return (xf * g.astype(jnp.float32)).astype(x.dtype)


def _swiglu(h, w_gate, w_up, w_down):
    return (jax.nn.silu(h @ w_gate) * (h @ w_up)) @ w_down


def _t5_bucket(n):
    max_exact = REL_BUCKETS // 2
    nf = jnp.maximum(n, max_exact).astype(jnp.float32)
    large = max_exact + (jnp.log(nf / max_exact) / math.log(REL_MAX_DIST / max_exact)
                         * (REL_BUCKETS - max_exact)).astype(jnp.int32)
    large = jnp.minimum(large, REL_BUCKETS - 1)
    return jnp.where(n < max_exact, n, large)


def _pool_mixer(u, w_groups, scale):
    s = u.shape[1]
    uf = u.astype(jnp.float32)
    cs = jnp.cumsum(uf, axis=1)
    t = jnp.arange(s)
    outs = []
    for g, w in enumerate(POOL_WINDOWS):
        sl = slice(g * POOL_GROUP_WIDTH, (g + 1) * POOL_GROUP_WIDTH)
        c = cs[..., sl]
        prev = jnp.pad(c, ((0, 0), (w, 0), (0, 0)))[:, :s]
        cnt = jnp.minimum(t + 1, w).astype(jnp.float32)[None, :, None]
        p = ((c - prev) / cnt - uf[..., sl]).astype(u.dtype)
        outs.append(p @ w_groups[g])
    return jnp.concatenate(outs, axis=-1) * scale


def _diff_attention(q, k, v, lam, rel_bias):
    s = q.shape[3]
    outs = []
    for i in range(s // Q_BLOCK):
        s0, e = i * Q_BLOCK, (i + 1) * Q_BLOCK
        qb = q[:, :, :, s0:e]
        kb = k[:, :, :, :e]
        vb = v[:, :, :e]
        scores = jnp.einsum('bhmqd,bhmkd->bhmqk', qb, kb).astype(jnp.float32)
        qpos = s0 + jnp.arange(Q_BLOCK)
        kpos = jnp.arange(e)
        dist = qpos[:, None] - kpos[None, :]
        bias = rel_bias.astype(jnp.float32)[_t5_bucket(jnp.maximum(dist, 0))]
        scores = scores + jnp.transpose(bias, (2, 0, 1))[None, :, None]
        scores = jnp.where((dist >= 0)[None, None, None], scores, NEG_INF)
        p = jax.nn.softmax(scores, axis=-1)
        a = p[:, :, 0] - lam * p[:, :, 1]
        outs.append(jnp.einsum('bhqk,bhkv->bhqv', a.astype(vb.dtype), vb))
    return jnp.concatenate(outs, axis=2)


def _even_mixer(h, layer_idx, w_in, pool_w, pool_scale, lq1, lk1, lq2, lk2, subln_g, w_out, rel_bias):
    b, s, _ = h.shape
    z = h @ w_in
    u = z[..., :POOL_WIDTH]
    o0 = POOL_WIDTH
    q = z[..., o0:o0 + ATTN_WIDTH].reshape(b, s, N_HEADS_DIFF, 2, DIFF_HEAD_DIM)
    k = z[..., o0 + ATTN_WIDTH:o0 + 2 * ATTN_WIDTH].reshape(b, s, N_HEADS_DIFF, 2, DIFF_HEAD_DIM)
    v = z[..., o0 + 2 * ATTN_WIDTH:].reshape(b, s, N_HEADS_DIFF, DIFF_V_DIM)
    q = jnp.transpose(q, (0, 2, 3, 1, 4)) * (DIFF_HEAD_DIM ** -0.5)
    k = jnp.transpose(k, (0, 2, 3, 1, 4))
    v = jnp.transpose(v, (0, 2, 1, 3))
    lam_init = 0.8 - 0.6 * math.exp(-0.3 * layer_idx)
    f32 = jnp.float32
    lam = (jnp.exp(jnp.sum(lq1.astype(f32) * lk1.astype(f32)))
           - jnp.exp(jnp.sum(lq2.astype(f32) * lk2.astype(f32))) + lam_init)
    att = _diff_attention(q, k, v, lam, rel_bias)
    att = _rmsnorm(att, subln_g) * (1.0 - lam_init)
    att = jnp.transpose(att, (0, 2, 1, 3)).reshape(b, s, ATTN_WIDTH)
    pool = _pool_mixer(u, pool_w, pool_scale)
    return jnp.concatenate([pool, att], axis=-1) @ w_out


def _conv_mixer(h, w_in, conv_w, w_out):
    d = h.shape[-1]
    z = h @ w_in
    gate_b, gate_c, xin = z[..., :d], z[..., d:2 * d], z[..., 2 * d:]
    u = gate_c * xin
    y = lax.conv_general_dilated(u, conv_w[:, None, :].astype(u.dtype), window_strides=(1,),
                                 padding=[(CONV_WIDTH - 1, 0)],
                                 dimension_numbers=('NWC', 'WIO', 'NWC'),
                                 feature_group_count=d)
    return (gate_b * y) @ w_out


def setup_inputs(seed: int = 0) -> dict:
    key = jax.random.key(seed)
    ks = jax.random.split(key, 24)
    f32 = jnp.float32
    nrm = lambda k, shape, s: jax.random.normal(k, shape, f32) * s
    return {
        "x": nrm(ks[0], (BATCH, SEQ, D_MODEL), 1.0),
        "ffn_norm_g": 1.0 + nrm(ks[1], (DEPTH, 2, D_MODEL), 0.02),
        "mix_norm_g": 1.0 + nrm(ks[2], (DEPTH, D_MODEL), 0.02),
        "final_norm_g": 1.0 + nrm(ks[3], (D_MODEL,), 0.02),
        "ffn_w_gate": nrm(ks[4], (DEPTH, 2, D_MODEL, D_FF), D_MODEL ** -0.5),
        "ffn_w_up": nrm(ks[5], (DEPTH, 2, D_MODEL, D_FF), D_MODEL ** -0.5),
        "ffn_w_down": nrm(ks[6], (DEPTH, 2, D_FF, D_MODEL), D_FF ** -0.5),
        "ab_w_in": nrm(ks[7], (N_EVEN, D_MODEL, AB_IN_WIDTH), D_MODEL ** -0.5),
        "pool_w": nrm(ks[8], (N_EVEN, POOL_GROUPS, POOL_GROUP_WIDTH, POOL_GROUP_WIDTH), POOL_GROUP_WIDTH ** -0.5),
        "pool_scale": 1.0 + nrm(ks[9], (N_EVEN, POOL_WIDTH), 0.02),
        "lam_q1": nrm(ks[10], (N_EVEN, DIFF_HEAD_DIM), 0.1),
        "lam_k1": nrm(ks[11], (N_EVEN, DIFF_HEAD_DIM), 0.1),
        "lam_q2": nrm(ks[12], (N_EVEN, DIFF_HEAD_DIM), 0.1),
        "lam_k2": nrm(ks[13], (N_EVEN, DIFF_HEAD_DIM), 0.1),
        "subln_g": 1.0 + nrm(ks[14], (N_EVEN, DIFF_V_DIM), 0.02),
        "ab_w_out": nrm(ks[15], (N_EVEN, MIX_WIDTH, D_MODEL), MIX_WIDTH ** -0.5),
        "rel_bias": nrm(ks[16], (REL_BUCKETS, N_HEADS_DIFF), 0.5),
        "conv_w_in": nrm(ks[17], (N_ODD, D_MODEL, 3 * D_MODEL), D_MODEL ** -0.5),
        "conv_w": nrm(ks[18], (N_ODD, CONV_WIDTH, D_MODEL), CONV_WIDTH ** -0.5),
        "conv_w_out": nrm(ks[19], (N_ODD, D_MODEL, D_MODEL), D_MODEL ** -0.5),
    }


def reference(x, ffn_norm_g, mix_norm_g, final_norm_g, ffn_w_gate, ffn_w_up, ffn_w_down,
              ab_w_in, pool_w, pool_scale, lam_q1, lam_k1, lam_q2, lam_k2, subln_g, ab_w_out,
              rel_bias, conv_w_in, conv_w, conv_w_out):
    for l in range(DEPTH):
        h = _rmsnorm(x, ffn_norm_g[l, 0])
        x = x + 0.5 * _swiglu(h, ffn_w_gate[l, 0], ffn_w_up[l, 0], ffn_w_down[l, 0])
        h = _rmsnorm(x, mix_norm_g[l])
        j = l // 2
        if l % 2 == 0:
            x = x + _even_mixer(h, l, ab_w_in[j], pool_w[j], pool_scale[j], lam_q1[j], lam_k1[j],
                                lam_q2[j], lam_k2[j], subln_g[j], ab_w_out[j], rel_bias)
        else:
            x = x + _conv_mixer(h, conv_w_in[j], conv_w[j], conv_w_out[j])
        h = _rmsnorm(x, ffn_norm_g[l, 1])
        x = x + 0.5 * _swiglu(h, ffn_w_gate[l, 1], ffn_w_up[l, 1], ffn_w_down[l, 1])
    return _rmsnorm(x, final_norm_g)
```

```python
import functools
import math

import jax
import jax.numpy as jnp
from jax import lax
from jax.experimental import pallas as pl
from jax.experimental.pallas import tpu as pltpu

F32 = jnp.float32
BF16 = jnp.bfloat16

NORM_EPS = 1e-6
NEG_INF = -1e30
POOL_WINDOWS = (2, 4, 8, 16)
POOL_GROUP_WIDTH = 256
DIFF_HEAD_DIM = 128
DIFF_V_DIM = 2 * DIFF_HEAD_DIM
REL_BUCKETS = 32
REL_MAX_DIST = 128

V7X_VMEM_BYTES = 64 * 1024 * 1024
VMEM_LIMIT_BYTES = V7X_VMEM_BYTES - 8 * 1024 * 1024

ATTN_BLOCK = 256
POOL_BLOCK = 256


def _params(*semantics):
    return pltpu.CompilerParams(dimension_semantics=semantics, vmem_limit_bytes=VMEM_LIMIT_BYTES)


def _rmsnorm_f32(x, g):
    return x * lax.rsqrt(jnp.mean(x * x, axis=-1, keepdims=True) + NORM_EPS) * g


def _ffn_kernel(x_ref, g_ref, wg_ref, wu_ref, wd_ref, o_ref, h_ref):
    @pl.when(pl.program_id(1) == 0)
    def _():
        x = x_ref[...]
        h_ref[...] = _rmsnorm_f32(x, g_ref[...]).astype(BF16)
        o_ref[...] = x

    h = h_ref[...]
    gate = jnp.dot(h, wg_ref[...], preferred_element_type=F32)
    up = jnp.dot(h, wu_ref[...], preferred_element_type=F32)
    act = (0.5 * gate * jax.nn.sigmoid(gate)) * up
    o_ref[...] += jnp.dot(act.astype(BF16), wd_ref[...], preferred_element_type=F32)


def _ffn(x, g, wg, wu, wd, *, tm, tf):
    m, d = x.shape
    f = wg.shape[1]
    return pl.pallas_call(
        _ffn_kernel,
        grid=(m // tm, f // tf),
        in_specs=[
            pl.BlockSpec((tm, d), lambda i, j: (i, 0)),
            pl.BlockSpec((1, d), lambda i, j: (0, 0)),
            pl.BlockSpec((d, tf), lambda i, j: (0, j)),
            pl.BlockSpec((d, tf), lambda i, j: (0, j)),
            pl.BlockSpec((tf, d), lambda i, j: (j, 0)),
        ],
        out_specs=pl.BlockSpec((tm, d), lambda i, j: (i, 0)),
        out_shape=jax.ShapeDtypeStruct((m, d), F32),
        scratch_shapes=[pltpu.VMEM((tm, d), BF16)],
        compiler_params=_params("parallel", "arbitrary"),
        name="ffn",
    )(x, g.reshape(1, d), wg, wu, wd)


def _norm_matmul_kernel(x_ref, g_ref, w_ref, o_ref, h_ref):
    @pl.when(pl.program_id(1) == 0)
    def _():
        h_ref[...] = _rmsnorm_f32(x_ref[...], g_ref[...]).astype(BF16)

    o_ref[...] = jnp.dot(h_ref[...], w_ref[...], preferred_element_type=F32).astype(o_ref.dtype)


def _norm_matmul(x, g, w, *, tm, tn):
    m, d = x.shape
    n = w.shape[1]
    return pl.pallas_call(
        _norm_matmul_kernel,
        grid=(m // tm, n // tn),
        in_specs=[
            pl.BlockSpec((tm, d), lambda i, j: (i, 0)),
            pl.BlockSpec((1, d), lambda i, j: (0, 0)),
            pl.BlockSpec((d, tn), lambda i, j: (0, j)),
        ],
        out_specs=pl.BlockSpec((tm, tn), lambda i, j: (i, j)),
        out_shape=jax.ShapeDtypeStruct((m, n), BF16),
        scratch_shapes=[pltpu.VMEM((tm, d), BF16)],
        compiler_params=_params("parallel", "arbitrary"),
        name="norm_matmul",
    )(x, g.reshape(1, d), w)


def _proj_residual_kernel(*refs, n_pairs):
    a_refs, w_refs = refs[:n_pairs], refs[n_pairs:2 * n_pairs]
    x_ref, o_ref = refs[2 * n_pairs], refs[2 * n_pairs + 1]
    acc = x_ref[...]
    for a_ref, w_ref in zip(a_refs, w_refs):
        acc += jnp.dot(a_ref[...], w_ref[...], preferred_element_type=F32)
    o_ref[...] = acc


def _proj_residual(acts, weights, x, *, tm, tn):
    m, n = x.shape
    in_specs = [pl.BlockSpec((tm, a.shape[1]), lambda i, j: (i, 0)) for a in acts]
    in_specs += [pl.BlockSpec((w.shape[0], tn), lambda i, j: (0, j)) for w in weights]
    in_specs += [pl.BlockSpec((tm, tn), lambda i, j: (i, j))]
    return pl.pallas_call(
        functools.partial(_proj_residual_kernel, n_pairs=len(acts)),
        grid=(m // tm, n // tn),
        in_specs=in_specs,
        out_specs=pl.BlockSpec((tm, tn), lambda i, j: (i, j)),
        out_shape=jax.ShapeDtypeStruct((m, n), F32),
        compiler_params=_params("parallel", "parallel"),
        name="proj_residual",
    )(*acts, *weights, x)


def _t5_bucket(n):
    max_exact = REL_BUCKETS // 2
    nf = jnp.maximum(n, max_exact).astype(F32)
    large = max_exact + (jnp.log(nf / max_exact) / math.log(REL_MAX_DIST / max_exact)
                         * (REL_BUCKETS - max_exact)).astype(jnp.int32)
    large = jnp.minimum(large, REL_BUCKETS - 1)
    return jnp.where(n < max_exact, n, large)


def _bias_tiles(rel_bias, blk):
    assert blk >= REL_MAX_DIST
    r = jnp.arange(blk)[:, None]
    c = jnp.arange(blk)[None, :]
    dist = jnp.stack([jnp.maximum(delta * blk + r - c, 0) for delta in range(3)])
    return jnp.transpose(rel_bias.astype(F32)[_t5_bucket(dist)], (3, 0, 1, 2))


def _attn_kernel(lam_ref, q_ref, k_ref, v_ref, bias_ref, sg_ref, o_ref, m_ref, l_ref, acc_ref, *, lam_init):
    blk = ATTN_BLOCK
    d = DIFF_HEAD_DIM
    i = pl.program_id(2)
    q = q_ref[0]
    col_minus_row = (lax.broadcasted_iota(jnp.int32, (blk, blk), 1)
                     - lax.broadcasted_iota(jnp.int32, (blk, blk), 0))
    m_ref[...] = jnp.full(m_ref.shape, NEG_INF, F32)
    l_ref[...] = jnp.zeros(l_ref.shape, F32)
    acc_ref[...] = jnp.zeros(acc_ref.shape, F32)

    def kv_step(j, carry):
        rows = pl.ds(pl.multiple_of(j * blk, blk), blk)
        k = k_ref[0, rows, :]
        v = v_ref[0, rows, :]
        bias = bias_ref[0, jnp.minimum(i - j, 2)]
        future = col_minus_row > (i - j) * blk
        for half in range(2):
            s = lax.dot_general(q[:, half * d:(half + 1) * d], k[:, half * d:(half + 1) * d],
                                (((1,), (1,)), ((), ())), preferred_element_type=F32)
            s = jnp.where(future, NEG_INF, s * (d ** -0.5) + bias)
            m_prev = m_ref[half]
            m_new = jnp.maximum(m_prev, jnp.max(s, axis=-1, keepdims=True))
            alpha = jnp.exp(m_prev - m_new)
            p = jnp.exp(s - m_new)
            l_ref[half] = alpha * l_ref[half] + jnp.sum(p, axis=-1, keepdims=True)
            acc_ref[half] = alpha * acc_ref[half] + jnp.dot(p.astype(BF16), v, preferred_element_type=F32)
            m_ref[half] = m_new
        return carry

    lax.fori_loop(0, i + 1, kv_step, 0)

    lam_p = lam_ref[...]
    lam = (jnp.exp(jnp.sum(lam_p[0:1] * lam_p[1:2], axis=-1, keepdims=True))
           - jnp.exp(jnp.sum(lam_p[2:3] * lam_p[3:4], axis=-1, keepdims=True)) + lam_init)
    att = acc_ref[0] / l_ref[0] - lam * (acc_ref[1] / l_ref[1])
    o_ref[0] = (_rmsnorm_f32(att, sg_ref[...]) * (1.0 - lam_init)).astype(o_ref.dtype)


def _diff_attention(z, lam_params, bias_tiles, subln_g, *, lam_init, n_heads, col_q, col_k, col_v):
    b, s, _ = z.shape
    blk = ATTN_BLOCK
    dv = DIFF_V_DIM
    return pl.pallas_call(
        functools.partial(_attn_kernel, lam_init=lam_init),
        grid=(b, n_heads, s // blk),
        in_specs=[
            pl.BlockSpec((4, DIFF_HEAD_DIM), lambda bi, h, i: (0, 0)),
            pl.BlockSpec((1, blk, dv), lambda bi, h, i: (bi, i, col_q + h)),
            pl.BlockSpec((1, s, dv), lambda bi, h, i: (bi, 0, col_k + h)),
            pl.BlockSpec((1, s, dv), lambda bi, h, i: (bi, 0, col_v + h)),
            pl.BlockSpec((1, 3, blk, blk), lambda bi, h, i: (h, 0, 0, 0)),
            pl.BlockSpec((1, dv), lambda bi, h, i: (0, 0)),
        ],
        out_specs=pl.BlockSpec((1, blk, dv), lambda bi, h, i: (bi, i, h)),
        out_shape=jax.ShapeDtypeStruct((b, s, n_heads * dv), BF16),
        scratch_shapes=[pltpu.VMEM((2, blk, 1), F32), pltpu.VMEM((2, blk, 1), F32),
                        pltpu.VMEM((2, blk, dv), F32)],
        compiler_params=_params("parallel", "parallel", "arbitrary"),
        name="diff_attention",
    )(lam_params, z, z, z, bias_tiles, subln_g.reshape(1, dv))


def _pool_kernel(cur_ref, prev_ref, pw_ref, sc_ref, o_ref):
    blk = POOL_BLOCK
    gw = POOL_GROUP_WIDTH
    c = pl.program_id(1)
    row = lax.broadcasted_iota(jnp.int32, (blk, blk), 0)
    col = lax.broadcasted_iota(jnp.int32, (blk, blk), 1)
    back = row - col
    t = c * blk + lax.broadcasted_iota(jnp.int32, (blk, 1), 0)
    has_prev = (c > 0).astype(F32)
    for g, w in enumerate(POOL_WINDOWS):
        cols = slice(g * gw, (g + 1) * gw)
        in_cur = jnp.where((back >= 0) & (back < w), 1.0, 0.0).astype(BF16)
        in_prev = jnp.where(back + blk < w, 1.0, 0.0).astype(BF16)
        cur = cur_ref[0, :, cols]
        window_sum = (jnp.dot(in_cur, cur, preferred_element_type=F32)
                      + has_prev * jnp.dot(in_prev, prev_ref[0, :, cols], preferred_element_type=F32))
        cnt = jnp.minimum(t + 1, w).astype(F32)
        p = window_sum / cnt - cur.astype(F32)
        y = jnp.dot(p.astype(BF16), pw_ref[g], preferred_element_type=F32) * sc_ref[:, cols]
        o_ref[0, :, cols] = y.astype(o_ref.dtype)


def _pool_mixer(z, pool_w, pool_scale):
    b, s, _ = z.shape
    blk = POOL_BLOCK
    pw = len(POOL_WINDOWS) * POOL_GROUP_WIDTH
    assert blk >= max(POOL_WINDOWS)
    return pl.pallas_call(
        _pool_kernel,
        grid=(b, s // blk),
        in_specs=[
            pl.BlockSpec((1, blk, pw), lambda bi, c: (bi, c, 0)),
            pl.BlockSpec((1, blk, pw), lambda bi, c: (bi, jnp.maximum(c - 1, 0), 0)),
            pl.BlockSpec(pool_w.shape, lambda bi, c: (0, 0, 0)),
            pl.BlockSpec((1, pw), lambda bi, c: (0, 0)),
        ],
        out_specs=pl.BlockSpec((1, blk, pw), lambda bi, c: (bi, c, 0)),
        out_shape=jax.ShapeDtypeStruct((b, s, pw), BF16),
        compiler_params=_params("parallel", "parallel"),
        name="pool_mixer",
    )(z, z, pool_w, pool_scale.reshape(1, pw))


def _conv_gate_kernel(gb_ref, gc_ref, xi_ref, cw_ref, o_ref):
    u = gc_ref[0].astype(F32) * xi_ref[0].astype(F32)
    row = lax.broadcasted_iota(jnp.int32, u.shape, 0)
    taps = cw_ref[...]
    y = taps[2:3] * u
    for back in (1, 2):
        shifted = jnp.where(row >= back, pltpu.roll(u, back, 0), 0.0)
        y += taps[2 - back:3 - back] * shifted
    o_ref[0] = (gb_ref[0].astype(F32) * y).astype(o_ref.dtype)


def _conv_gate(z, conv_w, *, cb):
    b, s, d3 = z.shape
    d = d3 // 3
    nb = d // cb
    return pl.pallas_call(
        _conv_gate_kernel,
        grid=(b, nb),
        in_specs=[
            pl.BlockSpec((1, s, cb), lambda bi, j: (bi, 0, j)),
            pl.BlockSpec((1, s, cb), lambda bi, j: (bi, 0, nb + j)),
            pl.BlockSpec((1, s, cb), lambda bi, j: (bi, 0, 2 * nb + j)),
            pl.BlockSpec((3, cb), lambda bi, j: (0, j)),
        ],
        out_specs=pl.BlockSpec((1, s, cb), lambda bi, j: (bi, 0, j)),
        out_shape=jax.ShapeDtypeStruct((b, s, d), BF16),
        compiler_params=_params("parallel", "parallel"),
        name="conv_gate",
    )(z, z, z, conv_w)


def _final_norm_kernel(x_ref, g_ref, o_ref):
    o_ref[...] = _rmsnorm_f32(x_ref[...], g_ref[...])


def _final_norm(x, g, *, tm):
    m, d = x.shape
    return pl.pallas_call(
        _final_norm_kernel,
        grid=(m // tm,),
        in_specs=[pl.BlockSpec((tm, d), lambda i: (i, 0)), pl.BlockSpec((1, d), lambda i: (0, 0))],
        out_specs=pl.BlockSpec((tm, d), lambda i: (i, 0)),
        out_shape=jax.ShapeDtypeStruct((m, d), F32),
        compiler_params=_params("parallel"),
        name="final_norm",
    )(x, g.reshape(1, d))


def kernel(x, ffn_norm_g, mix_norm_g, final_norm_g, ffn_w_gate, ffn_w_up, ffn_w_down, ab_w_in, pool_w, pool_scale, lam_q1, lam_k1, lam_q2, lam_k2, subln_g, ab_w_out, rel_bias, conv_w_in, conv_w, conv_w_out):
    b, s, d = x.shape
    depth = ffn_norm_g.shape[0]
    pool_width = pool_scale.shape[1]
    n_heads = rel_bias.shape[1]
    dv_blocks = pool_width // DIFF_V_DIM

    w_gate, w_up, w_down = (w.astype(BF16) for w in (ffn_w_gate, ffn_w_up, ffn_w_down))
    ab_in, ab_out, pool_wb = ab_w_in.astype(BF16), ab_w_out.astype(BF16), pool_w.astype(BF16)
    cv_in, cv_out = conv_w_in.astype(BF16), conv_w_out.astype(BF16)
    bias_tiles = _bias_tiles(rel_bias, ATTN_BLOCK)

    ffn = functools.partial(_ffn, tm=512, tf=512)
    xm = x.reshape(b * s, d)
    for l in range(depth):
        xm = ffn(xm, ffn_norm_g[l, 0], w_gate[l, 0], w_up[l, 0], w_down[l, 0])
        j = l // 2
        if l % 2 == 0:
            z = _norm_matmul(xm, mix_norm_g[l], ab_in[j], tm=1024, tn=1024).reshape(b, s, -1)
            lam_params = jnp.stack([lam_q1[j], lam_k1[j], lam_q2[j], lam_k2[j]]).astype(F32)
            att = _diff_attention(z, lam_params, bias_tiles, subln_g[j],
                                  lam_init=0.8 - 0.6 * math.exp(-0.3 * l), n_heads=n_heads,
                                  col_q=dv_blocks, col_k=dv_blocks + n_heads, col_v=dv_blocks + 2 * n_heads)
            pool = _pool_mixer(z, pool_wb[j], pool_scale[j])
            xm = _proj_residual([pool.reshape(b * s, -1), att.reshape(b * s, -1)],
                                [ab_out[j, :pool_width], ab_out[j, pool_width:]], xm, tm=1024, tn=1024)
        else:
            z = _norm_matmul(xm, mix_norm_g[l], cv_in[j], tm=1024, tn=1024).reshape(b, s, -1)
            gated = _conv_gate(z, conv_w[j], cb=512)
            xm = _proj_residual([gated.reshape(b * s, d)], [cv_out[j]], xm, tm=1024, tn=1024)
        xm = ffn(xm, ffn_norm_g[l, 1], w_gate[l, 1], w_up[l, 1], w_down[l, 1])
    return _final_norm(xm, final_norm_g, tm=512).reshape(b, s, d)
```

```python
import functools
import math

import jax
import jax.numpy as jnp
from jax import lax
from jax.experimental import pallas as pl
from jax.experimental.pallas import tpu as pltpu

F32 = jnp.float32
BF16 = jnp.bfloat16

NORM_EPS = 1e-6
NEG_INF = -1e30
POOL_WINDOWS = (2, 4, 8, 16)
POOL_GROUP_WIDTH = 256
DIFF_HEAD_DIM = 128
DIFF_V_DIM = 2 * DIFF_HEAD_DIM
REL_BUCKETS = 32
REL_MAX_DIST = 128

V7X_VMEM_BYTES = 64 * 1024 * 1024
VMEM_LIMIT_BYTES = V7X_VMEM_BYTES - 8 * 1024 * 1024
LANES = 128

ATTN_BLOCK = 256
POOL_BLOCK = 256


def _params(*semantics):
    return pltpu.CompilerParams(dimension_semantics=semantics, vmem_limit_bytes=VMEM_LIMIT_BYTES)


def _rmsnorm_f32(x, g):
    return x * lax.rsqrt(jnp.mean(x * x, axis=-1, keepdims=True) + NORM_EPS) * g


def _stacked_spec(lead, block, index_map):
    return pl.BlockSpec((None,) * len(lead) + block, lambda *g: lead + index_map(*g))


def _ffn_kernel(x_ref, g_ref, wg_ref, wu_ref, wd_ref, o_ref, h_ref):
    @pl.when(pl.program_id(1) == 0)
    def _():
        x = x_ref[...]
        h_ref[...] = _rmsnorm_f32(x, g_ref[...]).astype(BF16)
        o_ref[...] = x

    h = h_ref[...]
    gate = jnp.dot(h, wg_ref[...], preferred_element_type=F32)
    up = jnp.dot(h, wu_ref[...], preferred_element_type=F32)
    act = (0.5 * gate * jax.nn.sigmoid(gate)) * up
    o_ref[...] += jnp.dot(act.astype(BF16), wd_ref[...], preferred_element_type=F32)


def _ffn(x, g, wg, wu, wd, lead, *, tm, tf):
    m, d = x.shape
    f = wg.shape[-1]
    return pl.pallas_call(
        _ffn_kernel,
        grid=(m // tm, f // tf),
        in_specs=[
            pl.BlockSpec((tm, d), lambda i, j: (i, 0)),
            pl.BlockSpec((1, d), lambda i, j: (0, 0)),
            _stacked_spec(lead, (d, tf), lambda i, j: (0, j)),
            _stacked_spec(lead, (d, tf), lambda i, j: (0, j)),
            _stacked_spec(lead, (tf, d), lambda i, j: (j, 0)),
        ],
        out_specs=pl.BlockSpec((tm, d), lambda i, j: (i, 0)),
        out_shape=jax.ShapeDtypeStruct((m, d), F32),
        scratch_shapes=[pltpu.VMEM((tm, d), BF16)],
        compiler_params=_params("parallel", "arbitrary"),
        name="ffn",
    )(x, g.reshape(1, d), wg, wu, wd)


def _norm_matmul_kernel(x_ref, g_ref, w_ref, o_ref, h_ref):
    @pl.when(pl.program_id(1) == 0)
    def _():
        h_ref[...] = _rmsnorm_f32(x_ref[...], g_ref[...]).astype(BF16)

    o_ref[...] = jnp.dot(h_ref[...], w_ref[...], preferred_element_type=F32).astype(o_ref.dtype)


def _norm_matmul(x, g, w, lead, *, tm, tn):
    m, d = x.shape
    n = w.shape[-1]
    return pl.pallas_call(
        _norm_matmul_kernel,
        grid=(m // tm, n // tn),
        in_specs=[
            pl.BlockSpec((tm, d), lambda i, j: (i, 0)),
            pl.BlockSpec((1, d), lambda i, j: (0, 0)),
            _stacked_spec(lead, (d, tn), lambda i, j: (0, j)),
        ],
        out_specs=pl.BlockSpec((tm, tn), lambda i, j: (i, j)),
        out_shape=jax.ShapeDtypeStruct((m, n), BF16),
        scratch_shapes=[pltpu.VMEM((tm, d), BF16)],
        compiler_params=_params("parallel", "arbitrary"),
        name="norm_matmul",
    )(x, g.reshape(1, d), w)


def _proj_residual_kernel(*refs, n_acts):
    a_refs, w_refs = refs[:n_acts], refs[n_acts:2 * n_acts]
    x_ref, o_ref = refs[2 * n_acts], refs[2 * n_acts + 1]
    acc = x_ref[...]
    for a_ref, w_ref in zip(a_refs, w_refs):
        acc += jnp.dot(a_ref[...], w_ref[...], preferred_element_type=F32)
    o_ref[...] = acc


def _proj_residual(acts, w, lead, x, *, tm, tn):
    m, n = x.shape
    ka = acts[0].shape[1]
    assert all(a.shape[1] == ka for a in acts) and ka * len(acts) == w.shape[-2]
    in_specs = [pl.BlockSpec((tm, ka), lambda i, j: (i, 0)) for _ in acts]
    in_specs += [_stacked_spec(lead, (ka, tn), functools.partial(lambda i, j, p: (p, j), p=p))
                 for p in range(len(acts))]
    in_specs += [pl.BlockSpec((tm, tn), lambda i, j: (i, j))]
    return pl.pallas_call(
        functools.partial(_proj_residual_kernel, n_acts=len(acts)),
        grid=(m // tm, n // tn),
        in_specs=in_specs,
        out_specs=pl.BlockSpec((tm, tn), lambda i, j: (i, j)),
        out_shape=jax.ShapeDtypeStruct((m, n), F32),
        compiler_params=_params("parallel", "parallel"),
        name="proj_residual",
    )(*acts, *([w] * len(acts)), x)


def _t5_bucket(n):
    max_exact = REL_BUCKETS // 2
    nf = jnp.maximum(n, max_exact).astype(F32)
    large = max_exact + (jnp.log(nf / max_exact) / math.log(REL_MAX_DIST / max_exact)
                         * (REL_BUCKETS - max_exact)).astype(jnp.int32)
    large = jnp.minimum(large, REL_BUCKETS - 1)
    return jnp.where(n < max_exact, n, large)


def _bias_tile_kernel(rb_ref, o_ref):
    blk = ATTN_BLOCK
    h = pl.program_id(0)
    back = (lax.broadcasted_iota(jnp.int32, (blk, blk), 0)
            - lax.broadcasted_iota(jnp.int32, (blk, blk), 1))
    far = rb_ref[REL_BUCKETS - 1, h]
    for delta in range(2):
        bucket = _t5_bucket(jnp.maximum(delta * blk + back, 0))
        tile = jnp.zeros((blk, blk), F32)
        for b in range(REL_BUCKETS):
            tile = jnp.where(bucket == b, rb_ref[b, h] - far, tile)
        o_ref[0, delta] = tile


def _bias_tiles(rel_bias):
    blk = ATTN_BLOCK
    assert blk >= REL_MAX_DIST
    n_heads = rel_bias.shape[1]
    return pl.pallas_call(
        _bias_tile_kernel,
        grid=(n_heads,),
        in_specs=[pl.BlockSpec(memory_space=pltpu.SMEM)],
        out_specs=pl.BlockSpec((1, 2, blk, blk), lambda h: (h, 0, 0, 0)),
        out_shape=jax.ShapeDtypeStruct((n_heads, 2, blk, blk), F32),
        compiler_params=_params("parallel"),
        name="bias_tiles",
    )(rel_bias.astype(F32))


def _attn_kernel(lam_ref, q_ref, k_ref, v_ref, bias_ref, sg_ref, o_ref, *, lam_init):
    blk = ATTN_BLOCK
    d = DIFF_HEAD_DIM
    n_blocks = q_ref.shape[1] // blk
    future = (lax.broadcasted_iota(jnp.int32, (blk, blk), 1)
              > lax.broadcasted_iota(jnp.int32, (blk, blk), 0))
    lam_p = lam_ref[...]
    lam = (jnp.exp(jnp.sum(lam_p[0:1] * lam_p[1:2], axis=-1, keepdims=True))
           - jnp.exp(jnp.sum(lam_p[2:3] * lam_p[3:4], axis=-1, keepdims=True)) + lam_init)

    for c in range(n_blocks):
        n = (c + 1) * blk
        maps = []
        for half in range(2):
            cols = slice(half * d, (half + 1) * d)
            s = lax.dot_general(q_ref[0, c * blk:n, cols], k_ref[0, :n, cols],
                                (((1,), (1,)), ((), ())), preferred_element_type=F32) * (d ** -0.5)
            pieces = [jnp.where(future, NEG_INF, s[:, n - blk:] + bias_ref[0, 0])]
            if c >= 1:
                pieces.insert(0, s[:, n - 2 * blk:n - blk] + bias_ref[0, 1])
            if c >= 2:
                pieces.insert(0, s[:, :n - 2 * blk])
            s = jnp.concatenate(pieces, axis=1) if len(pieces) > 1 else pieces[0]
            p = jnp.exp(s - jnp.max(s, axis=-1, keepdims=True))
            inv_l = 1.0 / jnp.sum(p, axis=-1, keepdims=True)
            maps.append(jnp.dot(p.astype(BF16), v_ref[0, :n, :], preferred_element_type=F32) * inv_l)
        att = maps[0] - lam * maps[1]
        o_ref[0, c * blk:n, :] = (_rmsnorm_f32(att, sg_ref[...]) * (1.0 - lam_init)).astype(o_ref.dtype)


def _diff_attention(z, lam_params, bias_tiles, subln_g, *, lam_init, n_heads, col_q, col_k, col_v):
    b, s, _ = z.shape
    blk = ATTN_BLOCK
    dv = DIFF_V_DIM
    seq_spec = lambda col: pl.BlockSpec((1, s, dv), lambda bi, h: (bi, 0, col + h))
    return pl.pallas_call(
        functools.partial(_attn_kernel, lam_init=lam_init),
        grid=(b, n_heads),
        in_specs=[
            pl.BlockSpec((4, DIFF_HEAD_DIM), lambda bi, h: (0, 0)),
            seq_spec(col_q), seq_spec(col_k), seq_spec(col_v),
            pl.BlockSpec((1, 2, blk, blk), lambda bi, h: (h, 0, 0, 0)),
            pl.BlockSpec((1, dv), lambda bi, h: (0, 0)),
        ],
        out_specs=pl.BlockSpec((1, s, dv), lambda bi, h: (bi, 0, h)),
        out_shape=jax.ShapeDtypeStruct((b, s, n_heads * dv), BF16),
        compiler_params=_params("parallel", "parallel"),
        name="diff_attention",
    )(lam_params, z, z, z, bias_tiles, subln_g.reshape(1, dv))


def _pool_kernel(cur_ref, prev_ref, pw_ref, sc_ref, o_ref):
    blk = POOL_BLOCK
    gw = POOL_GROUP_WIDTH
    c = pl.program_id(1)
    row = lax.broadcasted_iota(jnp.int32, (blk, blk), 0)
    col = lax.broadcasted_iota(jnp.int32, (blk, blk), 1)
    back = row - col
    t = c * blk + lax.broadcasted_iota(jnp.int32, (blk, 1), 0)
    has_prev = (c > 0).astype(F32)
    for g, w in enumerate(POOL_WINDOWS):
        cols = slice(g * gw, (g + 1) * gw)
        in_cur = jnp.where((back >= 0) & (back < w), 1.0, 0.0).astype(BF16)
        in_prev = jnp.where(back + blk < w, 1.0, 0.0).astype(BF16)
        cur = cur_ref[0, :, cols]
        window_sum = (jnp.dot(in_cur, cur, preferred_element_type=F32)
                      + has_prev * jnp.dot(in_prev, prev_ref[0, :, cols], preferred_element_type=F32))
        cnt = jnp.minimum(t + 1, w).astype(F32)
        p = window_sum / cnt - cur.astype(F32)
        y = jnp.dot(p.astype(BF16), pw_ref[g], preferred_element_type=F32) * sc_ref[:, cols]
        o_ref[0, :, cols] = y.astype(o_ref.dtype)


def _pool_mixer(z, pool_w, lead, pool_scale):
    b, s, _ = z.shape
    blk = POOL_BLOCK
    pw = len(POOL_WINDOWS) * POOL_GROUP_WIDTH
    assert blk >= max(POOL_WINDOWS)
    return pl.pallas_call(
        _pool_kernel,
        grid=(b, s // blk),
        in_specs=[
            pl.BlockSpec((1, blk, pw), lambda bi, c: (bi, c, 0)),
            pl.BlockSpec((1, blk, pw), lambda bi, c: (bi, jnp.maximum(c - 1, 0), 0)),
            _stacked_spec(lead, pool_w.shape[-3:], lambda bi, c: (0, 0, 0)),
            pl.BlockSpec((1, pw), lambda bi, c: (0, 0)),
        ],
        out_specs=pl.BlockSpec((1, blk, pw), lambda bi, c: (bi, c, 0)),
        out_shape=jax.ShapeDtypeStruct((b, s, pw), BF16),
        compiler_params=_params("parallel", "parallel"),
        name="pool_mixer",
    )(z, z, pool_w, pool_scale.reshape(1, pw))


def _conv_gate_kernel(gb_ref, gc_ref, xi_ref, cw_ref, o_ref):
    u = gc_ref[0].astype(F32) * xi_ref[0].astype(F32)
    row = lax.broadcasted_iota(jnp.int32, u.shape, 0)
    taps = cw_ref[...]
    y = taps[2:3] * u
    for back in (1, 2):
        shifted = jnp.where(row >= back, pltpu.roll(u, back, 0), 0.0)
        y += taps[2 - back:3 - back] * shifted
    o_ref[0] = (gb_ref[0].astype(F32) * y).astype(o_ref.dtype)


def _conv_gate(z, conv_w, *, cb):
    b, s, d3 = z.shape
    d = d3 // 3
    nb = d // cb
    return pl.pallas_call(
        _conv_gate_kernel,
        grid=(b, nb),
        in_specs=[
            pl.BlockSpec((1, s, cb), lambda bi, j: (bi, 0, j)),
            pl.BlockSpec((1, s, cb), lambda bi, j: (bi, 0, nb + j)),
            pl.BlockSpec((1, s, cb), lambda bi, j: (bi, 0, 2 * nb + j)),
            pl.BlockSpec((3, cb), lambda bi, j: (0, j)),
        ],
        out_specs=pl.BlockSpec((1, s, cb), lambda bi, j: (bi, 0, j)),
        out_shape=jax.ShapeDtypeStruct((b, s, d), BF16),
        compiler_params=_params("parallel", "parallel"),
        name="conv_gate",
    )(z, z, z, conv_w)


def _final_norm_kernel(x_ref, g_ref, o_ref):
    o_ref[...] = _rmsnorm_f32(x_ref[...], g_ref[...])


def _final_norm(x, g, *, tm):
    m, d = x.shape
    return pl.pallas_call(
        _final_norm_kernel,
        grid=(m // tm,),
        in_specs=[pl.BlockSpec((tm, d), lambda i: (i, 0)), pl.BlockSpec((1, d), lambda i: (0, 0))],
        out_specs=pl.BlockSpec((tm, d), lambda i: (i, 0)),
        out_shape=jax.ShapeDtypeStruct((m, d), F32),
        compiler_params=_params("parallel"),
        name="final_norm",
    )(x, g.reshape(1, d))


def kernel(x, ffn_norm_g, mix_norm_g, final_norm_g, ffn_w_gate, ffn_w_up, ffn_w_down, ab_w_in, pool_w, pool_scale, lam_q1, lam_k1, lam_q2, lam_k2, subln_g, ab_w_out, rel_bias, conv_w_in, conv_w, conv_w_out):
    b, s, d = x.shape
    depth = ffn_norm_g.shape[0]
    pool_width = pool_scale.shape[1]
    n_heads = rel_bias.shape[1]
    dv_blocks = pool_width // DIFF_V_DIM

    w_gate, w_up, w_down = (w.astype(BF16) for w in (ffn_w_gate, ffn_w_up, ffn_w_down))
    ab_in, ab_out, pool_wb = ab_w_in.astype(BF16), ab_w_out.astype(BF16), pool_w.astype(BF16)
    cv_in, cv_out = conv_w_in.astype(BF16), conv_w_out.astype(BF16)
    bias_tiles = _bias_tiles(rel_bias)

    ffn = functools.partial(_ffn, tm=512, tf=512)
    xm = x.reshape(b * s, d)
    for l in range(depth):
        xm = ffn(xm, ffn_norm_g[l, 0], w_gate, w_up, w_down, (l, 0))
        j = l // 2
        if l % 2 == 0:
            z = _norm_matmul(xm, mix_norm_g[l], ab_in, (j,), tm=1024, tn=1024).reshape(b, s, -1)
            lam_params = jnp.stack([lam_q1[j], lam_k1[j], lam_q2[j], lam_k2[j]]).astype(F32)
            att = _diff_attention(z, lam_params, bias_tiles, subln_g[j],
                                  lam_init=0.8 - 0.6 * math.exp(-0.3 * l), n_heads=n_heads,
                                  col_q=dv_blocks, col_k=dv_blocks + n_heads, col_v=dv_blocks + 2 * n_heads)
            pool = _pool_mixer(z, pool_wb, (j,), pool_scale[j])
            xm = _proj_residual([pool.reshape(b * s, -1), att.reshape(b * s, -1)], ab_out, (j,), xm,
                                tm=1024, tn=1024)
        else:
            z = _norm_matmul(xm, mix_norm_g[l], cv_in, (j,), tm=1024, tn=1024).reshape(b, s, -1)
            gated = _conv_gate(z, conv_w[j], cb=512)
            xm = _proj_residual([gated.reshape(b * s, d)], cv_out, (j,), xm, tm=1024, tn=1024)
        xm = ffn(xm, ffn_norm_g[l, 1], w_gate, w_up, w_down, (l, 1))
    return _final_norm(xm, final_norm_g, tm=512).reshape(b, s, d)
```

```python
import functools
import math

import jax
import jax.numpy as jnp
from jax import lax
from jax.experimental import pallas as pl
from jax.experimental.pallas import tpu as pltpu

F32 = jnp.float32
BF16 = jnp.bfloat16

NORM_EPS = 1e-6
NEG_INF = -1e30
POOL_WINDOWS = (2, 4, 8, 16)
POOL_GROUP_WIDTH = 256
DIFF_HEAD_DIM = 128
DIFF_V_DIM = 2 * DIFF_HEAD_DIM
REL_BUCKETS = 32
REL_MAX_DIST = 128

V7X_VMEM_BYTES = 64 * 1024 * 1024
VMEM_LIMIT_BYTES = V7X_VMEM_BYTES - 4 * 1024 * 1024
LANES = 128

ATTN_BLOCK = 256
POOL_BLOCK = 256


def _params(*semantics):
    return pltpu.CompilerParams(dimension_semantics=semantics, vmem_limit_bytes=VMEM_LIMIT_BYTES)


def _rmsnorm_f32(x, g):
    return x * lax.rsqrt(jnp.mean(x * x, axis=-1, keepdims=True) + NORM_EPS) * g


def _stacked_spec(lead, block, index_map):
    return pl.BlockSpec((None,) * len(lead) + block, lambda *g: lead + index_map(*g))


def _ffn_kernel(x_ref, g_ref, wg_ref, wu_ref, wd_ref, og_ref, o_ref, h_ref, *, norm_output):
    f = pl.program_id(1)

    @pl.when(f == 0)
    def _():
        x = x_ref[...]
        h_ref[...] = _rmsnorm_f32(x, g_ref[...]).astype(BF16)
        o_ref[...] = x

    h = h_ref[...]
    gate = jnp.dot(h, wg_ref[...], preferred_element_type=F32)
    up = jnp.dot(h, wu_ref[...], preferred_element_type=F32)
    act = (0.5 * gate * jax.nn.sigmoid(gate)) * up
    o_ref[...] += jnp.dot(act.astype(BF16), wd_ref[...], preferred_element_type=F32)

    if norm_output:
        @pl.when(f == pl.num_programs(1) - 1)
        def _():
            o_ref[...] = _rmsnorm_f32(o_ref[...], og_ref[...])


def _ffn(x, g, wg, wu, wd, lead, out_g, *, norm_output, tm, tf):
    m, d = x.shape
    f = wg.shape[-1]
    return pl.pallas_call(
        functools.partial(_ffn_kernel, norm_output=norm_output),
        grid=(m // tm, f // tf),
        in_specs=[
            pl.BlockSpec((tm, d), lambda i, j: (i, 0)),
            pl.BlockSpec((1, d), lambda i, j: (0, 0)),
            _stacked_spec(lead, (d, tf), lambda i, j: (0, j)),
            _stacked_spec(lead, (d, tf), lambda i, j: (0, j)),
            _stacked_spec(lead, (tf, d), lambda i, j: (j, 0)),
            pl.BlockSpec((1, d), lambda i, j: (0, 0)),
        ],
        out_specs=pl.BlockSpec((tm, d), lambda i, j: (i, 0)),
        out_shape=jax.ShapeDtypeStruct((m, d), F32),
        scratch_shapes=[pltpu.VMEM((tm, d), BF16)],
        compiler_params=_params("parallel", "arbitrary"),
        name="ffn",
    )(x, g.reshape(1, d), wg, wu, wd, out_g.reshape(1, d))


def _norm_matmul_kernel(x_ref, g_ref, w_ref, o_ref, h_ref):
    @pl.when(pl.program_id(1) == 0)
    def _():
        h_ref[...] = _rmsnorm_f32(x_ref[...], g_ref[...]).astype(BF16)

    o_ref[...] = jnp.dot(h_ref[...], w_ref[...], preferred_element_type=F32).astype(o_ref.dtype)


def _norm_matmul(x, g, w, lead, *, tm, tn):
    m, d = x.shape
    n = w.shape[-1]
    return pl.pallas_call(
        _norm_matmul_kernel,
        grid=(m // tm, n // tn),
        in_specs=[
            pl.BlockSpec((tm, d), lambda i, j: (i, 0)),
            pl.BlockSpec((1, d), lambda i, j: (0, 0)),
            _stacked_spec(lead, (d, tn), lambda i, j: (0, j)),
        ],
        out_specs=pl.BlockSpec((tm, tn), lambda i, j: (i, j)),
        out_shape=jax.ShapeDtypeStruct((m, n), BF16),
        scratch_shapes=[pltpu.VMEM((tm, d), BF16)],
        compiler_params=_params("parallel", "arbitrary"),
        name="norm_matmul",
    )(x, g.reshape(1, d), w)


def _proj_residual_kernel(*refs, n_acts):
    a_refs, w_refs = refs[:n_acts], refs[n_acts:2 * n_acts]
    x_ref, o_ref = refs[2 * n_acts], refs[2 * n_acts + 1]
    acc = x_ref[...]
    for a_ref, w_ref in zip(a_refs, w_refs):
        acc += jnp.dot(a_ref[...], w_ref[...], preferred_element_type=F32)
    o_ref[...] = acc


def _proj_residual(acts, w, lead, x, *, tm, tn):
    m, n = x.shape
    ka = acts[0].shape[1]
    assert all(a.shape[1] == ka for a in acts) and ka * len(acts) == w.shape[-2]
    in_specs = [pl.BlockSpec((tm, ka), lambda i, j: (i, 0)) for _ in acts]
    in_specs += [_stacked_spec(lead, (ka, tn), functools.partial(lambda i, j, p: (p, j), p=p))
                 for p in range(len(acts))]
    in_specs += [pl.BlockSpec((tm, tn), lambda i, j: (i, j))]
    return pl.pallas_call(
        functools.partial(_proj_residual_kernel, n_acts=len(acts)),
        grid=(m // tm, n // tn),
        in_specs=in_specs,
        out_specs=pl.BlockSpec((tm, tn), lambda i, j: (i, j)),
        out_shape=jax.ShapeDtypeStruct((m, n), F32),
        compiler_params=_params("parallel", "parallel"),
        name="proj_residual",
    )(*acts, *([w] * len(acts)), x)


def _t5_bucket(n):
    max_exact = REL_BUCKETS // 2
    nf = jnp.maximum(n, max_exact).astype(F32)
    large = max_exact + (jnp.log(nf / max_exact) / math.log(REL_MAX_DIST / max_exact)
                         * (REL_BUCKETS - max_exact)).astype(jnp.int32)
    large = jnp.minimum(large, REL_BUCKETS - 1)
    return jnp.where(n < max_exact, n, large)


def _bias_tile_kernel(rb_ref, o_ref):
    blk = ATTN_BLOCK
    h = pl.program_id(0)
    back = (lax.broadcasted_iota(jnp.int32, (blk, blk), 0)
            - lax.broadcasted_iota(jnp.int32, (blk, blk), 1))
    far = rb_ref[REL_BUCKETS - 1, h]
    for delta in range(2):
        bucket = _t5_bucket(jnp.maximum(delta * blk + back, 0))
        tile = jnp.zeros((blk, blk), F32)
        for b in range(REL_BUCKETS):
            tile = jnp.where(bucket == b, rb_ref[b, h] - far, tile)
        o_ref[0, delta] = tile


def _bias_tiles(rel_bias):
    blk = ATTN_BLOCK
    assert blk >= REL_MAX_DIST
    n_heads = rel_bias.shape[1]
    return pl.pallas_call(
        _bias_tile_kernel,
        grid=(n_heads,),
        in_specs=[pl.BlockSpec(memory_space=pltpu.SMEM)],
        out_specs=pl.BlockSpec((1, 2, blk, blk), lambda h: (h, 0, 0, 0)),
        out_shape=jax.ShapeDtypeStruct((n_heads, 2, blk, blk), F32),
        compiler_params=_params("parallel"),
        name="bias_tiles",
    )(rel_bias.astype(F32))


def _attn_kernel(lam_ref, q_ref, k_ref, v_ref, bias_ref, sg_ref, o_ref, *, lam_init):
    blk = ATTN_BLOCK
    d = DIFF_HEAD_DIM
    n_blocks = q_ref.shape[1] // blk
    future = (lax.broadcasted_iota(jnp.int32, (blk, blk), 1)
              > lax.broadcasted_iota(jnp.int32, (blk, blk), 0))
    lam_p = lam_ref[...]
    lam = (jnp.exp(jnp.sum(lam_p[0:1] * lam_p[1:2], axis=-1, keepdims=True))
           - jnp.exp(jnp.sum(lam_p[2:3] * lam_p[3:4], axis=-1, keepdims=True)) + lam_init)

    for c in range(n_blocks):
        n = (c + 1) * blk
        maps = []
        for half in range(2):
            cols = slice(half * d, (half + 1) * d)
            s = lax.dot_general(q_ref[0, c * blk:n, cols], k_ref[0, :n, cols],
                                (((1,), (1,)), ((), ())), preferred_element_type=F32) * (d ** -0.5)
            pieces = [jnp.where(future, NEG_INF, s[:, n - blk:] + bias_ref[0, 0])]
            if c >= 1:
                pieces.insert(0, s[:, n - 2 * blk:n - blk] + bias_ref[0, 1])
            if c >= 2:
                pieces.insert(0, s[:, :n - 2 * blk])
            s = jnp.concatenate(pieces, axis=1) if len(pieces) > 1 else pieces[0]
            p = jnp.exp(s - jnp.max(s, axis=-1, keepdims=True))
            inv_l = 1.0 / jnp.sum(p, axis=-1, keepdims=True)
            maps.append(jnp.dot(p.astype(BF16), v_ref[0, :n, :], preferred_element_type=F32) * inv_l)
        att = maps[0] - lam * maps[1]
        o_ref[0, c * blk:n, :] = (_rmsnorm_f32(att, sg_ref[...]) * (1.0 - lam_init)).astype(o_ref.dtype)


def _diff_attention(z, lam_params, bias_tiles, subln_g, *, lam_init, n_heads, col_q, col_k, col_v):
    b, s, _ = z.shape
    blk = ATTN_BLOCK
    dv = DIFF_V_DIM
    seq_spec = lambda col: pl.BlockSpec((1, s, dv), lambda bi, h: (bi, 0, col + h))
    return pl.pallas_call(
        functools.partial(_attn_kernel, lam_init=lam_init),
        grid=(b, n_heads),
        in_specs=[
            pl.BlockSpec((4, DIFF_HEAD_DIM), lambda bi, h: (0, 0)),
            seq_spec(col_q), seq_spec(col_k), seq_spec(col_v),
            pl.BlockSpec((1, 2, blk, blk), lambda bi, h: (h, 0, 0, 0)),
            pl.BlockSpec((1, dv), lambda bi, h: (0, 0)),
        ],
        out_specs=pl.BlockSpec((1, s, dv), lambda bi, h: (bi, 0, h)),
        out_shape=jax.ShapeDtypeStruct((b, s, n_heads * dv), BF16),
        compiler_params=_params("parallel", "parallel"),
        name="diff_attention",
    )(lam_params, z, z, z, bias_tiles, subln_g.reshape(1, dv))


def _pool_kernel(cur_ref, prev_ref, pw_ref, sc_ref, o_ref):
    blk = POOL_BLOCK
    gw = POOL_GROUP_WIDTH
    c = pl.program_id(1)
    row = lax.broadcasted_iota(jnp.int32, (blk, blk), 0)
    col = lax.broadcasted_iota(jnp.int32, (blk, blk), 1)
    back = row - col
    t = c * blk + lax.broadcasted_iota(jnp.int32, (blk, 1), 0)
    has_prev = (c > 0).astype(F32)
    for g, w in enumerate(POOL_WINDOWS):
        cols = slice(g * gw, (g + 1) * gw)
        in_cur = jnp.where((back >= 0) & (back < w), 1.0, 0.0).astype(BF16)
        in_prev = jnp.where(back + blk < w, 1.0, 0.0).astype(BF16)
        cur = cur_ref[0, :, cols]
        window_sum = (jnp.dot(in_cur, cur, preferred_element_type=F32)
                      + has_prev * jnp.dot(in_prev, prev_ref[0, :, cols], preferred_element_type=F32))
        cnt = jnp.minimum(t + 1, w).astype(F32)
        p = window_sum / cnt - cur.astype(F32)
        y = jnp.dot(p.astype(BF16), pw_ref[g], preferred_element_type=F32) * sc_ref[:, cols]
        o_ref[0, :, cols] = y.astype(o_ref.dtype)


def _pool_mixer(z, pool_w, lead, pool_scale):
    b, s, _ = z.shape
    blk = POOL_BLOCK
    pw = len(POOL_WINDOWS) * POOL_GROUP_WIDTH
    assert blk >= max(POOL_WINDOWS)
    return pl.pallas_call(
        _pool_kernel,
        grid=(b, s // blk),
        in_specs=[
            pl.BlockSpec((1, blk, pw), lambda bi, c: (bi, c, 0)),
            pl.BlockSpec((1, blk, pw), lambda bi, c: (bi, jnp.maximum(c - 1, 0), 0)),
            _stacked_spec(lead, pool_w.shape[-3:], lambda bi, c: (0, 0, 0)),
            pl.BlockSpec((1, pw), lambda bi, c: (0, 0)),
        ],
        out_specs=pl.BlockSpec((1, blk, pw), lambda bi, c: (bi, c, 0)),
        out_shape=jax.ShapeDtypeStruct((b, s, pw), BF16),
        compiler_params=_params("parallel", "parallel"),
        name="pool_mixer",
    )(z, z, pool_w, pool_scale.reshape(1, pw))


def _conv_gate_kernel(gb_ref, gc_ref, xi_ref, cw_ref, o_ref):
    u = gc_ref[0].astype(F32) * xi_ref[0].astype(F32)
    row = lax.broadcasted_iota(jnp.int32, u.shape, 0)
    taps = cw_ref[...]
    y = taps[2:3] * u
    for back in (1, 2):
        shifted = jnp.where(row >= back, pltpu.roll(u, back, 0), 0.0)
        y += taps[2 - back:3 - back] * shifted
    o_ref[0] = (gb_ref[0].astype(F32) * y).astype(o_ref.dtype)


def _conv_gate(z, conv_w, *, cb):
    b, s, d3 = z.shape
    d = d3 // 3
    nb = d // cb
    return pl.pallas_call(
        _conv_gate_kernel,
        grid=(b, nb),
        in_specs=[
            pl.BlockSpec((1, s, cb), lambda bi, j: (bi, 0, j)),
            pl.BlockSpec((1, s, cb), lambda bi, j: (bi, 0, nb + j)),
            pl.BlockSpec((1, s, cb), lambda bi, j: (bi, 0, 2 * nb + j)),
            pl.BlockSpec((3, cb), lambda bi, j: (0, j)),
        ],
        out_specs=pl.BlockSpec((1, s, cb), lambda bi, j: (bi, 0, j)),
        out_shape=jax.ShapeDtypeStruct((b, s, d), BF16),
        compiler_params=_params("parallel", "parallel"),
        name="conv_gate",
    )(z, z, z, conv_w)


def kernel(x, ffn_norm_g, mix_norm_g, final_norm_g, ffn_w_gate, ffn_w_up, ffn_w_down, ab_w_in, pool_w, pool_scale, lam_q1, lam_k1, lam_q2, lam_k2, subln_g, ab_w_out, rel_bias, conv_w_in, conv_w, conv_w_out):
    b, s, d = x.shape
    depth = ffn_norm_g.shape[0]
    pool_width = pool_scale.shape[1]
    n_heads = rel_bias.shape[1]
    dv_blocks = pool_width // DIFF_V_DIM

    w_gate, w_up, w_down = (w.astype(BF16) for w in (ffn_w_gate, ffn_w_up, ffn_w_down))
    ab_in, ab_out, pool_wb = ab_w_in.astype(BF16), ab_w_out.astype(BF16), pool_w.astype(BF16)
    cv_in, cv_out = conv_w_in.astype(BF16), conv_w_out.astype(BF16)
    bias_tiles = _bias_tiles(rel_bias)

    ffn = functools.partial(_ffn, tm=1024, tf=512)
    xm = x.reshape(b * s, d)
    for l in range(depth):
        xm = ffn(xm, ffn_norm_g[l, 0], w_gate, w_up, w_down, (l, 0), final_norm_g, norm_output=False)
        j = l // 2
        if l % 2 == 0:
            z = _norm_matmul(xm, mix_norm_g[l], ab_in, (j,), tm=1024, tn=1024).reshape(b, s, -1)
            lam_params = jnp.stack([lam_q1[j], lam_k1[j], lam_q2[j], lam_k2[j]]).astype(F32)
            att = _diff_attention(z, lam_params, bias_tiles, subln_g[j],
                                  lam_init=0.8 - 0.6 * math.exp(-0.3 * l), n_heads=n_heads,
                                  col_q=dv_blocks, col_k=dv_blocks + n_heads, col_v=dv_blocks + 2 * n_heads)
            pool = _pool_mixer(z, pool_wb, (j,), pool_scale[j])
            xm = _proj_residual([pool.reshape(b * s, -1), att.reshape(b * s, -1)], ab_out, (j,), xm,
                                tm=512, tn=d)
        else:
            z = _norm_matmul(xm, mix_norm_g[l], cv_in, (j,), tm=1024, tn=1024).reshape(b, s, -1)
            gated = _conv_gate(z, conv_w[j], cb=512)
            xm = _proj_residual([gated.reshape(b * s, d)], cv_out, (j,), xm, tm=512, tn=d)
        xm = ffn(xm, ffn_norm_g[l, 1], w_gate, w_up, w_down, (l, 1), final_norm_g,
                 norm_output=(l == depth - 1))
    return xm.reshape(b, s, d)
```

```python
import functools
import math

import jax
import jax.numpy as jnp
from jax import lax
from jax.experimental import pallas as pl
from jax.experimental.pallas import tpu as pltpu

F32 = jnp.float32
BF16 = jnp.bfloat16

NORM_EPS = 1e-6
NEG_INF = -1e30
POOL_WINDOWS = (2, 4, 8, 16)
POOL_GROUP_WIDTH = 256
DIFF_HEAD_DIM = 128
DIFF_V_DIM = 2 * DIFF_HEAD_DIM
REL_BUCKETS = 32
REL_MAX_DIST = 128

V7X_VMEM_BYTES = 64 * 1024 * 1024
VMEM_LIMIT_BYTES = V7X_VMEM_BYTES - 4 * 1024 * 1024
LANES = 128

ATTN_BLOCK = 256
POOL_BLOCK = 256


def _params(*semantics):
    return pltpu.CompilerParams(dimension_semantics=semantics, vmem_limit_bytes=VMEM_LIMIT_BYTES)


def _rmsnorm_f32(x, g):
    return x * lax.rsqrt(jnp.mean(x * x, axis=-1, keepdims=True) + NORM_EPS) * g


def _stacked_spec(lead, block, index_map):
    return pl.BlockSpec((None,) * len(lead) + block, lambda *g: lead + index_map(*g))


def _ffn_kernel(x_ref, g_ref, wg_ref, wu_ref, wd_ref, og_ref, o_ref, h_ref, *, norm_output):
    f = pl.program_id(1)

    @pl.when(f == 0)
    def _():
        x = x_ref[...]
        h_ref[...] = _rmsnorm_f32(x, g_ref[...]).astype(BF16)
        o_ref[...] = x

    h = h_ref[...]
    gate = jnp.dot(h, wg_ref[...], preferred_element_type=F32)
    up = jnp.dot(h, wu_ref[...], preferred_element_type=F32)
    act = (0.5 * gate * jax.nn.sigmoid(gate)) * up
    o_ref[...] += jnp.dot(act.astype(BF16), wd_ref[...], preferred_element_type=F32)

    if norm_output:
        @pl.when(f == pl.num_programs(1) - 1)
        def _():
            o_ref[...] = _rmsnorm_f32(o_ref[...], og_ref[...])


def _ffn(x, g, wg, wu, wd, lead, out_g, *, norm_output, tm, tf):
    m, d = x.shape
    f = wg.shape[-1]
    return pl.pallas_call(
        functools.partial(_ffn_kernel, norm_output=norm_output),
        grid=(m // tm, f // tf),
        in_specs=[
            pl.BlockSpec((tm, d), lambda i, j: (i, 0)),
            pl.BlockSpec((1, d), lambda i, j: (0, 0)),
            _stacked_spec(lead, (d, tf), lambda i, j: (0, j)),
            _stacked_spec(lead, (d, tf), lambda i, j: (0, j)),
            _stacked_spec(lead, (tf, d), lambda i, j: (j, 0)),
            pl.BlockSpec((1, d), lambda i, j: (0, 0)),
        ],
        out_specs=pl.BlockSpec((tm, d), lambda i, j: (i, 0)),
        out_shape=jax.ShapeDtypeStruct((m, d), F32),
        scratch_shapes=[pltpu.VMEM((tm, d), BF16)],
        compiler_params=_params("parallel", "arbitrary"),
        name="ffn",
    )(x, g.reshape(1, d), wg, wu, wd, out_g.reshape(1, d))


def _norm_matmul_kernel(x_ref, g_ref, w_ref, o_ref, h_ref):
    @pl.when(pl.program_id(1) == 0)
    def _():
        h_ref[...] = _rmsnorm_f32(x_ref[...], g_ref[...]).astype(BF16)

    o_ref[...] = jnp.dot(h_ref[...], w_ref[...], preferred_element_type=F32).astype(o_ref.dtype)


def _norm_matmul(x, g, w, lead, *, tm, tn):
    m, d = x.shape
    n = w.shape[-1]
    return pl.pallas_call(
        _norm_matmul_kernel,
        grid=(m // tm, n // tn),
        in_specs=[
            pl.BlockSpec((tm, d), lambda i, j: (i, 0)),
            pl.BlockSpec((1, d), lambda i, j: (0, 0)),
            _stacked_spec(lead, (d, tn), lambda i, j: (0, j)),
        ],
        out_specs=pl.BlockSpec((tm, tn), lambda i, j: (i, j)),
        out_shape=jax.ShapeDtypeStruct((m, n), BF16),
        scratch_shapes=[pltpu.VMEM((tm, d), BF16)],
        compiler_params=_params("parallel", "arbitrary"),
        name="norm_matmul",
    )(x, g.reshape(1, d), w)


def _proj_residual_kernel(*refs, n_acts):
    a_refs, w_refs = refs[:n_acts], refs[n_acts:2 * n_acts]
    x_ref, o_ref = refs[2 * n_acts], refs[2 * n_acts + 1]
    acc = x_ref[...]
    for a_ref, w_ref in zip(a_refs, w_refs):
        acc += jnp.dot(a_ref[...], w_ref[...], preferred_element_type=F32)
    o_ref[...] = acc


def _proj_residual(acts, w, lead, x, *, tm, tn):
    m, n = x.shape
    ka = acts[0].shape[1]
    assert all(a.shape[1] == ka for a in acts) and ka * len(acts) == w.shape[-2]
    in_specs = [pl.BlockSpec((tm, ka), lambda i, j: (i, 0)) for _ in acts]
    in_specs += [_stacked_spec(lead, (ka, tn), functools.partial(lambda i, j, p: (p, j), p=p))
                 for p in range(len(acts))]
    in_specs += [pl.BlockSpec((tm, tn), lambda i, j: (i, j))]
    return pl.pallas_call(
        functools.partial(_proj_residual_kernel, n_acts=len(acts)),
        grid=(m // tm, n // tn),
        in_specs=in_specs,
        out_specs=pl.BlockSpec((tm, tn), lambda i, j: (i, j)),
        out_shape=jax.ShapeDtypeStruct((m, n), F32),
        compiler_params=_params("parallel", "parallel"),
        name="proj_residual",
    )(*acts, *([w] * len(acts)), x)


def _t5_bucket(n):
    max_exact = REL_BUCKETS // 2
    nf = jnp.maximum(n, max_exact).astype(F32)
    large = max_exact + (jnp.log(nf / max_exact) / math.log(REL_MAX_DIST / max_exact)
                         * (REL_BUCKETS - max_exact)).astype(jnp.int32)
    large = jnp.minimum(large, REL_BUCKETS - 1)
    return jnp.where(n < max_exact, n, large)


def _bias_tile_kernel(rb_ref, o_ref):
    blk = ATTN_BLOCK
    h = pl.program_id(0)
    back = (lax.broadcasted_iota(jnp.int32, (blk, blk), 0)
            - lax.broadcasted_iota(jnp.int32, (blk, blk), 1))
    far = rb_ref[REL_BUCKETS - 1, h]
    for delta in range(2):
        bucket = _t5_bucket(jnp.maximum(delta * blk + back, 0))
        tile = jnp.zeros((blk, blk), F32)
        for b in range(REL_BUCKETS):
            tile = jnp.where(bucket == b, rb_ref[b, h] - far, tile)
        o_ref[0, delta] = tile


def _bias_tiles(rel_bias):
    blk = ATTN_BLOCK
    assert blk >= REL_MAX_DIST
    n_heads = rel_bias.shape[1]
    return pl.pallas_call(
        _bias_tile_kernel,
        grid=(n_heads,),
        in_specs=[pl.BlockSpec(memory_space=pltpu.SMEM)],
        out_specs=pl.BlockSpec((1, 2, blk, blk), lambda h: (h, 0, 0, 0)),
        out_shape=jax.ShapeDtypeStruct((n_heads, 2, blk, blk), F32),
        compiler_params=_params("parallel"),
        name="bias_tiles",
    )(rel_bias.astype(F32))


def _attn_kernel(lam_ref, q_ref, k_ref, v_ref, bias_ref, sg_ref, o_ref, *, lam_init):
    blk = ATTN_BLOCK
    d = DIFF_HEAD_DIM
    n_blocks = q_ref.shape[1] // blk
    future = (lax.broadcasted_iota(jnp.int32, (blk, blk), 1)
              > lax.broadcasted_iota(jnp.int32, (blk, blk), 0))
    lam_p = lam_ref[...]
    lam = (jnp.exp(jnp.sum(lam_p[0:1] * lam_p[1:2], axis=-1, keepdims=True))
           - jnp.exp(jnp.sum(lam_p[2:3] * lam_p[3:4], axis=-1, keepdims=True)) + lam_init)

    for c in range(n_blocks):
        n = (c + 1) * blk
        maps = []
        for half in range(2):
            cols = slice(half * d, (half + 1) * d)
            s = lax.dot_general(q_ref[0, c * blk:n, cols], k_ref[0, :n, cols],
                                (((1,), (1,)), ((), ())), preferred_element_type=F32) * (d ** -0.5)
            pieces = [jnp.where(future, NEG_INF, s[:, n - blk:] + bias_ref[0, 0])]
            if c >= 1:
                pieces.insert(0, s[:, n - 2 * blk:n - blk] + bias_ref[0, 1])
            if c >= 2:
                pieces.insert(0, s[:, :n - 2 * blk])
            s = jnp.concatenate(pieces, axis=1) if len(pieces) > 1 else pieces[0]
            p = jnp.exp(s - jnp.max(s, axis=-1, keepdims=True))
            inv_l = 1.0 / jnp.sum(p, axis=-1, keepdims=True)
            maps.append(jnp.dot(p.astype(BF16), v_ref[0, :n, :], preferred_element_type=F32) * inv_l)
        att = maps[0] - lam * maps[1]
        o_ref[0, c * blk:n, :] = (_rmsnorm_f32(att, sg_ref[...]) * (1.0 - lam_init)).astype(o_ref.dtype)


def _diff_attention(z, lam_params, bias_tiles, subln_g, *, lam_init, n_heads, col_q, col_k, col_v):
    b, s, _ = z.shape
    blk = ATTN_BLOCK
    dv = DIFF_V_DIM
    seq_spec = lambda col: pl.BlockSpec((1, s, dv), lambda bi, h: (bi, 0, col + h))
    return pl.pallas_call(
        functools.partial(_attn_kernel, lam_init=lam_init),
        grid=(b, n_heads),
        in_specs=[
            pl.BlockSpec((4, DIFF_HEAD_DIM), lambda bi, h: (0, 0)),
            seq_spec(col_q), seq_spec(col_k), seq_spec(col_v),
            pl.BlockSpec((1, 2, blk, blk), lambda bi, h: (h, 0, 0, 0)),
            pl.BlockSpec((1, dv), lambda bi, h: (0, 0)),
        ],
        out_specs=pl.BlockSpec((1, s, dv), lambda bi, h: (bi, 0, h)),
        out_shape=jax.ShapeDtypeStruct((b, s, n_heads * dv), BF16),
        compiler_params=_params("parallel", "parallel"),
        name="diff_attention",
    )(lam_params, z, z, z, bias_tiles, subln_g.reshape(1, dv))


def _pool_kernel(cur_ref, prev_ref, pw_ref, sc_ref, o_ref):
    blk = POOL_BLOCK
    gw = POOL_GROUP_WIDTH
    c = pl.program_id(1)
    row = lax.broadcasted_iota(jnp.int32, (blk, blk), 0)
    col = lax.broadcasted_iota(jnp.int32, (blk, blk), 1)
    back = row - col
    t = c * blk + lax.broadcasted_iota(jnp.int32, (blk, 1), 0)
    has_prev = (c > 0).astype(F32)
    for g, w in enumerate(POOL_WINDOWS):
        cols = slice(g * gw, (g + 1) * gw)
        in_cur = jnp.where((back >= 0) & (back < w), 1.0, 0.0).astype(BF16)
        in_prev = jnp.where(back + blk < w, 1.0, 0.0).astype(BF16)
        cur = cur_ref[0, :, cols]
        window_sum = (jnp.dot(in_cur, cur, preferred_element_type=F32)
                      + has_prev * jnp.dot(in_prev, prev_ref[0, :, cols], preferred_element_type=F32))
        cnt = jnp.minimum(t + 1, w).astype(F32)
        p = window_sum / cnt - cur.astype(F32)
        y = jnp.dot(p.astype(BF16), pw_ref[g], preferred_element_type=F32) * sc_ref[:, cols]
        o_ref[0, :, cols] = y.astype(o_ref.dtype)


def _pool_mixer(z, pool_w, lead, pool_scale):
    b, s, _ = z.shape
    blk = POOL_BLOCK
    pw = len(POOL_WINDOWS) * POOL_GROUP_WIDTH
    assert blk >= max(POOL_WINDOWS)
    return pl.pallas_call(
        _pool_kernel,
        grid=(b, s // blk),
        in_specs=[
            pl.BlockSpec((1, blk, pw), lambda bi, c: (bi, c, 0)),
            pl.BlockSpec((1, blk, pw), lambda bi, c: (bi, jnp.maximum(c - 1, 0), 0)),
            _stacked_spec(lead, pool_w.shape[-3:], lambda bi, c: (0, 0, 0)),
            pl.BlockSpec((1, pw), lambda bi, c: (0, 0)),
        ],
        out_specs=pl.BlockSpec((1, blk, pw), lambda bi, c: (bi, c, 0)),
        out_shape=jax.ShapeDtypeStruct((b, s, pw), BF16),
        compiler_params=_params("parallel", "parallel"),
        name="pool_mixer",
    )(z, z, pool_w, pool_scale.reshape(1, pw))


CONV_HALO = 8


def _norm_conv_in_kernel(x_ref, g_ref, wb_ref, wc_ref, wx_ref, cw_ref, o_ref, h_ref, tail_ref, *, tiles_per_seq):
    i, j = pl.program_id(0), pl.program_id(1)
    tm = x_ref.shape[0]

    @pl.when(j == 0)
    def _():
        h_ref[...] = _rmsnorm_f32(x_ref[...], g_ref[...]).astype(BF16)

    @pl.when(i % tiles_per_seq == 0)
    def _():
        tail_ref[j] = jnp.zeros(tail_ref.shape[1:], F32)

    h = h_ref[...]
    gate_b = jnp.dot(h, wb_ref[...], preferred_element_type=F32)
    u = (jnp.dot(h, wc_ref[...], preferred_element_type=F32)
         * jnp.dot(h, wx_ref[...], preferred_element_type=F32))
    u_ext = jnp.concatenate([tail_ref[j], u], axis=0)
    tail_ref[j] = u[tm - CONV_HALO:, :]
    taps = cw_ref[...]
    y = taps[2:3] * u
    for back in (1, 2):
        y += taps[2 - back:3 - back] * pltpu.roll(u_ext, back, 0)[CONV_HALO:, :]
    o_ref[...] = (gate_b * y).astype(o_ref.dtype)


def _norm_conv_in(x, g, w, lead, conv_w, *, seq, tm, tc):
    m, d = x.shape
    nb = d // tc
    assert seq % tm == 0 and CONV_HALO >= conv_w.shape[0] - 1
    w_spec = lambda part: _stacked_spec(lead, (d, tc), lambda i, j: (0, part * nb + j))
    return pl.pallas_call(
        functools.partial(_norm_conv_in_kernel, tiles_per_seq=seq // tm),
        grid=(m // tm, nb),
        in_specs=[
            pl.BlockSpec((tm, d), lambda i, j: (i, 0)),
            pl.BlockSpec((1, d), lambda i, j: (0, 0)),
            w_spec(0), w_spec(1), w_spec(2),
            pl.BlockSpec((conv_w.shape[0], tc), lambda i, j: (0, j)),
        ],
        out_specs=pl.BlockSpec((tm, tc), lambda i, j: (i, j)),
        out_shape=jax.ShapeDtypeStruct((m, d), BF16),
        scratch_shapes=[pltpu.VMEM((tm, d), BF16), pltpu.VMEM((nb, CONV_HALO, tc), F32)],
        compiler_params=_params("arbitrary", "arbitrary"),
        name="norm_conv_in",
    )(x, g.reshape(1, d), w, w, w, conv_w)


def kernel(x, ffn_norm_g, mix_norm_g, final_norm_g, ffn_w_gate, ffn_w_up, ffn_w_down, ab_w_in, pool_w, pool_scale, lam_q1, lam_k1, lam_q2, lam_k2, subln_g, ab_w_out, rel_bias, conv_w_in, conv_w, conv_w_out):
    b, s, d = x.shape
    depth = ffn_norm_g.shape[0]
    pool_width = pool_scale.shape[1]
    n_heads = rel_bias.shape[1]
    dv_blocks = pool_width // DIFF_V_DIM

    w_gate, w_up, w_down = (w.astype(BF16) for w in (ffn_w_gate, ffn_w_up, ffn_w_down))
    ab_in, ab_out, pool_wb = ab_w_in.astype(BF16), ab_w_out.astype(BF16), pool_w.astype(BF16)
    cv_in, cv_out = conv_w_in.astype(BF16), conv_w_out.astype(BF16)
    bias_tiles = _bias_tiles(rel_bias)

    ffn = functools.partial(_ffn, tm=1024, tf=512)
    xm = x.reshape(b * s, d)
    for l in range(depth):
        xm = ffn(xm, ffn_norm_g[l, 0], w_gate, w_up, w_down, (l, 0), final_norm_g, norm_output=False)
        j = l // 2
        if l % 2 == 0:
            z = _norm_matmul(xm, mix_norm_g[l], ab_in, (j,), tm=1024, tn=1024).reshape(b, s, -1)
            lam_params = jnp.stack([lam_q1[j], lam_k1[j], lam_q2[j], lam_k2[j]]).astype(F32)
            att = _diff_attention(z, lam_params, bias_tiles, subln_g[j],
                                  lam_init=0.8 - 0.6 * math.exp(-0.3 * l), n_heads=n_heads,
                                  col_q=dv_blocks, col_k=dv_blocks + n_heads, col_v=dv_blocks + 2 * n_heads)
            pool = _pool_mixer(z, pool_wb, (j,), pool_scale[j])
            xm = _proj_residual([pool.reshape(b * s, -1), att.reshape(b * s, -1)], ab_out, (j,), xm,
                                tm=512, tn=d)
        else:
            gated = _norm_conv_in(xm, mix_norm_g[l], cv_in, (j,), conv_w[j], seq=s, tm=1024, tc=512)
            xm = _proj_residual([gated], cv_out, (j,), xm, tm=512, tn=d)
        xm = ffn(xm, ffn_norm_g[l, 1], w_gate, w_up, w_down, (l, 1), final_norm_g,
                 norm_output=(l == depth - 1))
    return xm.reshape(b, s, d)
```

```python
import functools
import math

import jax
import jax.numpy as jnp
from jax import lax
from jax.experimental import pallas as pl
from jax.experimental.pallas import tpu as pltpu

F32 = jnp.float32
BF16 = jnp.bfloat16

NORM_EPS = 1e-6
NEG_INF = -1e30
POOL_WINDOWS = (2, 4, 8, 16)
POOL_GROUP_WIDTH = 256
DIFF_HEAD_DIM = 128
DIFF_V_DIM = 2 * DIFF_HEAD_DIM
REL_BUCKETS = 32
REL_MAX_DIST = 128

V7X_VMEM_BYTES = 64 * 1024 * 1024
VMEM_LIMIT_BYTES = V7X_VMEM_BYTES - 2 * 1024 * 1024
LANES = 128

ATTN_BLOCK = 256
POOL_BLOCK = 256


def _params(*semantics):
    return pltpu.CompilerParams(dimension_semantics=semantics, vmem_limit_bytes=VMEM_LIMIT_BYTES)


def _rmsnorm_f32(x, g):
    return x * lax.rsqrt(jnp.mean(x * x, axis=-1, keepdims=True) + NORM_EPS) * g


def _stacked_spec(lead, block, index_map):
    return pl.BlockSpec((None,) * len(lead) + block, lambda *g: lead + index_map(*g))


def _ffn_kernel(x_ref, g_ref, wg_ref, wu_ref, wd_ref, og_ref, o_ref, h_ref, *, norm_output):
    f = pl.program_id(1)

    @pl.when(f == 0)
    def _():
        x = x_ref[...]
        h_ref[...] = _rmsnorm_f32(x, g_ref[...]).astype(BF16)
        o_ref[...] = x

    h = h_ref[...]
    gate = jnp.dot(h, wg_ref[...].astype(BF16), preferred_element_type=F32)
    up = jnp.dot(h, wu_ref[...].astype(BF16), preferred_element_type=F32)
    act = (0.5 * gate * jax.nn.sigmoid(gate)) * up
    o_ref[...] += jnp.dot(act.astype(BF16), wd_ref[...].astype(BF16), preferred_element_type=F32)

    if norm_output:
        @pl.when(f == pl.num_programs(1) - 1)
        def _():
            o_ref[...] = _rmsnorm_f32(o_ref[...], og_ref[...])


def _ffn(x, g, wg, wu, wd, lead, out_g, *, norm_output, tm, tf):
    m, d = x.shape
    f = wg.shape[-1]
    return pl.pallas_call(
        functools.partial(_ffn_kernel, norm_output=norm_output),
        grid=(m // tm, f // tf),
        in_specs=[
            pl.BlockSpec((tm, d), lambda i, j: (i, 0)),
            pl.BlockSpec((1, d), lambda i, j: (0, 0)),
            _stacked_spec(lead, (d, tf), lambda i, j: (0, j)),
            _stacked_spec(lead, (d, tf), lambda i, j: (0, j)),
            _stacked_spec(lead, (tf, d), lambda i, j: (j, 0)),
            pl.BlockSpec((1, d), lambda i, j: (0, 0)),
        ],
        out_specs=pl.BlockSpec((tm, d), lambda i, j: (i, 0)),
        out_shape=jax.ShapeDtypeStruct((m, d), F32),
        scratch_shapes=[pltpu.VMEM((tm, d), BF16)],
        compiler_params=_params("parallel", "arbitrary"),
        name="ffn",
    )(x, g.reshape(1, d), wg, wu, wd, out_g.reshape(1, d))


def _norm_matmul_kernel(x_ref, g_ref, w_ref, o_ref, h_ref):
    @pl.when(pl.program_id(1) == 0)
    def _():
        h_ref[...] = _rmsnorm_f32(x_ref[...], g_ref[...]).astype(BF16)

    o_ref[...] = jnp.dot(h_ref[...], w_ref[...], preferred_element_type=F32).astype(o_ref.dtype)


def _norm_matmul(x, g, w, lead, *, tm, tn):
    m, d = x.shape
    n = w.shape[-1]
    return pl.pallas_call(
        _norm_matmul_kernel,
        grid=(m // tm, n // tn),
        in_specs=[
            pl.BlockSpec((tm, d), lambda i, j: (i, 0)),
            pl.BlockSpec((1, d), lambda i, j: (0, 0)),
            _stacked_spec(lead, (d, tn), lambda i, j: (0, j)),
        ],
        out_specs=pl.BlockSpec((tm, tn), lambda i, j: (i, j)),
        out_shape=jax.ShapeDtypeStruct((m, n), BF16),
        scratch_shapes=[pltpu.VMEM((tm, d), BF16)],
        compiler_params=_params("parallel", "arbitrary"),
        name="norm_matmul",
    )(x, g.reshape(1, d), w)


def _proj_residual_kernel(*refs, n_acts):
    a_refs, w_refs = refs[:n_acts], refs[n_acts:2 * n_acts]
    x_ref, o_ref = refs[2 * n_acts], refs[2 * n_acts + 1]
    acc = x_ref[...]
    for a_ref, w_ref in zip(a_refs, w_refs):
        acc += jnp.dot(a_ref[...], w_ref[...], preferred_element_type=F32)
    o_ref[...] = acc


def _proj_residual(acts, w, lead, x, *, tm, tn):
    m, n = x.shape
    ka = acts[0].shape[1]
    assert all(a.shape[1] == ka for a in acts) and ka * len(acts) == w.shape[-2]
    in_specs = [pl.BlockSpec((tm, ka), lambda i, j: (i, 0)) for _ in acts]
    in_specs += [_stacked_spec(lead, (ka, tn), functools.partial(lambda i, j, p: (p, j), p=p))
                 for p in range(len(acts))]
    in_specs += [pl.BlockSpec((tm, tn), lambda i, j: (i, j))]
    return pl.pallas_call(
        functools.partial(_proj_residual_kernel, n_acts=len(acts)),
        grid=(m // tm, n // tn),
        in_specs=in_specs,
        out_specs=pl.BlockSpec((tm, tn), lambda i, j: (i, j)),
        out_shape=jax.ShapeDtypeStruct((m, n), F32),
        compiler_params=_params("parallel", "parallel"),
        name="proj_residual",
    )(*acts, *([w] * len(acts)), x)


def _t5_bucket(n):
    max_exact = REL_BUCKETS // 2
    nf = jnp.maximum(n, max_exact).astype(F32)
    large = max_exact + (jnp.log(nf / max_exact) / math.log(REL_MAX_DIST / max_exact)
                         * (REL_BUCKETS - max_exact)).astype(jnp.int32)
    large = jnp.minimum(large, REL_BUCKETS - 1)
    return jnp.where(n < max_exact, n, large)


def _bias_tile_kernel(rb_ref, o_ref):
    blk = ATTN_BLOCK
    h = pl.program_id(0)
    back = (lax.broadcasted_iota(jnp.int32, (blk, blk), 0)
            - lax.broadcasted_iota(jnp.int32, (blk, blk), 1))
    far = rb_ref[REL_BUCKETS - 1, h]
    for delta in range(2):
        bucket = _t5_bucket(jnp.maximum(delta * blk + back, 0))
        tile = jnp.zeros((blk, blk), F32)
        for b in range(REL_BUCKETS):
            tile = jnp.where(bucket == b, rb_ref[b, h] - far, tile)
        o_ref[0, delta] = tile


def _bias_tiles(rel_bias):
    blk = ATTN_BLOCK
    assert blk >= REL_MAX_DIST
    n_heads = rel_bias.shape[1]
    return pl.pallas_call(
        _bias_tile_kernel,
        grid=(n_heads,),
        in_specs=[pl.BlockSpec(memory_space=pltpu.SMEM)],
        out_specs=pl.BlockSpec((1, 2, blk, blk), lambda h: (h, 0, 0, 0)),
        out_shape=jax.ShapeDtypeStruct((n_heads, 2, blk, blk), F32),
        compiler_params=_params("parallel"),
        name="bias_tiles",
    )(rel_bias.astype(F32))


def _attn_kernel(lam_ref, q_ref, k_ref, v_ref, bias_ref, sg_ref, o_ref, *, lam_init):
    blk = ATTN_BLOCK
    d = DIFF_HEAD_DIM
    n_blocks = q_ref.shape[1] // blk
    future = (lax.broadcasted_iota(jnp.int32, (blk, blk), 1)
              > lax.broadcasted_iota(jnp.int32, (blk, blk), 0))
    lam_p = lam_ref[...]
    lam = (jnp.exp(jnp.sum(lam_p[0:1] * lam_p[1:2], axis=-1, keepdims=True))
           - jnp.exp(jnp.sum(lam_p[2:3] * lam_p[3:4], axis=-1, keepdims=True)) + lam_init)

    for c in range(n_blocks):
        n = (c + 1) * blk
        maps = []
        for half in range(2):
            cols = slice(half * d, (half + 1) * d)
            s = lax.dot_general(q_ref[0, c * blk:n, cols], k_ref[0, :n, cols],
                                (((1,), (1,)), ((), ())), preferred_element_type=F32) * (d ** -0.5)
            pieces = [jnp.where(future, NEG_INF, s[:, n - blk:] + bias_ref[0, 0])]
            if c >= 1:
                pieces.insert(0, s[:, n - 2 * blk:n - blk] + bias_ref[0, 1])
            if c >= 2:
                pieces.insert(0, s[:, :n - 2 * blk])
            s = jnp.concatenate(pieces, axis=1) if len(pieces) > 1 else pieces[0]
            p = jnp.exp(s - jnp.max(s, axis=-1, keepdims=True))
            inv_l = 1.0 / jnp.sum(p, axis=-1, keepdims=True)
            maps.append(jnp.dot(p.astype(BF16), v_ref[0, :n, :], preferred_element_type=F32) * inv_l)
        att = maps[0] - lam * maps[1]
        o_ref[0, c * blk:n, :] = (_rmsnorm_f32(att, sg_ref[...]) * (1.0 - lam_init)).astype(o_ref.dtype)


def _diff_attention(z, lam_params, bias_tiles, subln_g, *, lam_init, n_heads, col_q, col_k, col_v):
    b, s, _ = z.shape
    blk = ATTN_BLOCK
    dv = DIFF_V_DIM
    seq_spec = lambda col: pl.BlockSpec((1, s, dv), lambda bi, h: (bi, 0, col + h))
    return pl.pallas_call(
        functools.partial(_attn_kernel, lam_init=lam_init),
        grid=(b, n_heads),
        in_specs=[
            pl.BlockSpec((4, DIFF_HEAD_DIM), lambda bi, h: (0, 0)),
            seq_spec(col_q), seq_spec(col_k), seq_spec(col_v),
            pl.BlockSpec((1, 2, blk, blk), lambda bi, h: (h, 0, 0, 0)),
            pl.BlockSpec((1, dv), lambda bi, h: (0, 0)),
        ],
        out_specs=pl.BlockSpec((1, s, dv), lambda bi, h: (bi, 0, h)),
        out_shape=jax.ShapeDtypeStruct((b, s, n_heads * dv), BF16),
        compiler_params=_params("parallel", "parallel"),
        name="diff_attention",
    )(lam_params, z, z, z, bias_tiles, subln_g.reshape(1, dv))


def _pool_kernel(cur_ref, prev_ref, pw_ref, sc_ref, o_ref):
    blk = POOL_BLOCK
    gw = POOL_GROUP_WIDTH
    c = pl.program_id(1)
    row = lax.broadcasted_iota(jnp.int32, (blk, blk), 0)
    col = lax.broadcasted_iota(jnp.int32, (blk, blk), 1)
    back = row - col
    t = c * blk + lax.broadcasted_iota(jnp.int32, (blk, 1), 0)
    has_prev = (c > 0).astype(F32)
    for g, w in enumerate(POOL_WINDOWS):
        cols = slice(g * gw, (g + 1) * gw)
        in_cur = jnp.where((back >= 0) & (back < w), 1.0, 0.0).astype(BF16)
        in_prev = jnp.where(back + blk < w, 1.0, 0.0).astype(BF16)
        cur = cur_ref[0, :, cols]
        window_sum = (jnp.dot(in_cur, cur, preferred_element_type=F32)
                      + has_prev * jnp.dot(in_prev, prev_ref[0, :, cols], preferred_element_type=F32))
        cnt = jnp.minimum(t + 1, w).astype(F32)
        p = window_sum / cnt - cur.astype(F32)
        y = jnp.dot(p.astype(BF16), pw_ref[g], preferred_element_type=F32) * sc_ref[:, cols]
        o_ref[0, :, cols] = y.astype(o_ref.dtype)


def _pool_mixer(z, pool_w, lead, pool_scale):
    b, s, _ = z.shape
    blk = POOL_BLOCK
    pw = len(POOL_WINDOWS) * POOL_GROUP_WIDTH
    assert blk >= max(POOL_WINDOWS)
    return pl.pallas_call(
        _pool_kernel,
        grid=(b, s // blk),
        in_specs=[
            pl.BlockSpec((1, blk, pw), lambda bi, c: (bi, c, 0)),
            pl.BlockSpec((1, blk, pw), lambda bi, c: (bi, jnp.maximum(c - 1, 0), 0)),
            _stacked_spec(lead, pool_w.shape[-3:], lambda bi, c: (0, 0, 0)),
            pl.BlockSpec((1, pw), lambda bi, c: (0, 0)),
        ],
        out_specs=pl.BlockSpec((1, blk, pw), lambda bi, c: (bi, c, 0)),
        out_shape=jax.ShapeDtypeStruct((b, s, pw), BF16),
        compiler_params=_params("parallel", "parallel"),
        name="pool_mixer",
    )(z, z, pool_w, pool_scale.reshape(1, pw))


CONV_HALO = 8


def _norm_conv_in_kernel(x_ref, g_ref, wb_ref, wc_ref, wx_ref, cw_ref, o_ref, h_ref, tail_ref, *, tiles_per_seq):
    i, j = pl.program_id(0), pl.program_id(1)
    tm = x_ref.shape[0]

    @pl.when(j == 0)
    def _():
        h_ref[...] = _rmsnorm_f32(x_ref[...], g_ref[...]).astype(BF16)

    @pl.when(i % tiles_per_seq == 0)
    def _():
        tail_ref[j] = jnp.zeros(tail_ref.shape[1:], F32)

    h = h_ref[...]
    gate_b = jnp.dot(h, wb_ref[...], preferred_element_type=F32)
    u = (jnp.dot(h, wc_ref[...], preferred_element_type=F32)
         * jnp.dot(h, wx_ref[...], preferred_element_type=F32))
    u_ext = jnp.concatenate([tail_ref[j], u], axis=0)
    tail_ref[j] = u[tm - CONV_HALO:, :]
    taps = cw_ref[...]
    y = taps[2:3] * u
    for back in (1, 2):
        y += taps[2 - back:3 - back] * pltpu.roll(u_ext, back, 0)[CONV_HALO:, :]
    o_ref[...] = (gate_b * y).astype(o_ref.dtype)


def _norm_conv_in(x, g, w, lead, conv_w, *, seq, tm, tc):
    m, d = x.shape
    nb = d // tc
    assert seq % tm == 0 and CONV_HALO >= conv_w.shape[0] - 1
    w_spec = lambda part: _stacked_spec(lead, (d, tc), lambda i, j: (0, part * nb + j))
    return pl.pallas_call(
        functools.partial(_norm_conv_in_kernel, tiles_per_seq=seq // tm),
        grid=(m // tm, nb),
        in_specs=[
            pl.BlockSpec((tm, d), lambda i, j: (i, 0)),
            pl.BlockSpec((1, d), lambda i, j: (0, 0)),
            w_spec(0), w_spec(1), w_spec(2),
            pl.BlockSpec((conv_w.shape[0], tc), lambda i, j: (0, j)),
        ],
        out_specs=pl.BlockSpec((tm, tc), lambda i, j: (i, j)),
        out_shape=jax.ShapeDtypeStruct((m, d), BF16),
        scratch_shapes=[pltpu.VMEM((tm, d), BF16), pltpu.VMEM((nb, CONV_HALO, tc), F32)],
        compiler_params=_params("arbitrary", "arbitrary"),
        name="norm_conv_in",
    )(x, g.reshape(1, d), w, w, w, conv_w)


def kernel(x, ffn_norm_g, mix_norm_g, final_norm_g, ffn_w_gate, ffn_w_up, ffn_w_down, ab_w_in, pool_w, pool_scale, lam_q1, lam_k1, lam_q2, lam_k2, subln_g, ab_w_out, rel_bias, conv_w_in, conv_w, conv_w_out):
    b, s, d = x.shape
    depth = ffn_norm_g.shape[0]
    pool_width = pool_scale.shape[1]
    n_heads = rel_bias.shape[1]
    dv_blocks = pool_width // DIFF_V_DIM

    w_gate, w_up, w_down = ffn_w_gate.astype(BF16), ffn_w_up.astype(BF16), ffn_w_down
    ab_in, ab_out, pool_wb = ab_w_in.astype(BF16), ab_w_out.astype(BF16), pool_w.astype(BF16)
    cv_in, cv_out = conv_w_in.astype(BF16), conv_w_out.astype(BF16)
    bias_tiles = _bias_tiles(rel_bias)

    ffn = functools.partial(_ffn, tm=1024, tf=512)
    xm = x.reshape(b * s, d)
    for l in range(depth):
        xm = ffn(xm, ffn_norm_g[l, 0], w_gate, w_up, w_down, (l, 0), final_norm_g, norm_output=False)
        j = l // 2
        if l % 2 == 0:
            z = _norm_matmul(xm, mix_norm_g[l], ab_in, (j,), tm=1024, tn=1024).reshape(b, s, -1)
            lam_params = jnp.stack([lam_q1[j], lam_k1[j], lam_q2[j], lam_k2[j]]).astype(F32)
            att = _diff_attention(z, lam_params, bias_tiles, subln_g[j],
                                  lam_init=0.8 - 0.6 * math.exp(-0.3 * l), n_heads=n_heads,
                                  col_q=dv_blocks, col_k=dv_blocks + n_heads, col_v=dv_blocks + 2 * n_heads)
            pool = _pool_mixer(z, pool_wb, (j,), pool_scale[j])
            xm = _proj_residual([pool.reshape(b * s, -1), att.reshape(b * s, -1)], ab_out, (j,), xm,
                                tm=512, tn=d)
        else:
            gated = _norm_conv_in(xm, mix_norm_g[l], cv_in, (j,), conv_w[j], seq=s, tm=1024, tc=512)
            xm = _proj_residual([gated], cv_out, (j,), xm, tm=512, tn=d)
        xm = ffn(xm, ffn_norm_g[l, 1], w_gate, w_up, w_down, (l, 1), final_norm_g,
                 norm_output=(l == depth - 1))
    return xm.reshape(b, s, d)
```

```python
import functools
import math

import jax
import jax.numpy as jnp
from jax import lax
from jax.experimental import pallas as pl
from jax.experimental.pallas import tpu as pltpu

F32 = jnp.float32
BF16 = jnp.bfloat16

NORM_EPS = 1e-6
NEG_INF = -1e30
POOL_WINDOWS = (2, 4, 8, 16)
POOL_GROUP_WIDTH = 256
DIFF_HEAD_DIM = 128
DIFF_V_DIM = 2 * DIFF_HEAD_DIM
REL_BUCKETS = 32
REL_MAX_DIST = 128

V7X_VMEM_BYTES = 64 * 1024 * 1024
VMEM_LIMIT_BYTES = V7X_VMEM_BYTES - 2 * 1024 * 1024
LANES = 128

ATTN_BLOCK = 256
POOL_BLOCK = 256


def _params(*semantics):
    return pltpu.CompilerParams(dimension_semantics=semantics, vmem_limit_bytes=VMEM_LIMIT_BYTES)


def _rmsnorm_f32(x, g):
    return x * lax.rsqrt(jnp.mean(x * x, axis=-1, keepdims=True) + NORM_EPS) * g


def _stacked_spec(lead, block, index_map):
    return pl.BlockSpec((None,) * len(lead) + block, lambda *g: lead + index_map(*g))


def _ffn_kernel(x_ref, g_ref, wg_ref, wu_ref, wd_ref, og_ref, o_ref, h_ref, *, norm_output):
    f = pl.program_id(1)

    @pl.when(f == 0)
    def _():
        x = x_ref[...]
        h_ref[...] = _rmsnorm_f32(x, g_ref[...]).astype(BF16)
        o_ref[...] = x

    h = h_ref[...]
    gate = jnp.dot(h, wg_ref[...].astype(BF16), preferred_element_type=F32)
    up = jnp.dot(h, wu_ref[...].astype(BF16), preferred_element_type=F32)
    act = (0.5 * gate * jax.nn.sigmoid(gate)) * up
    o_ref[...] += jnp.dot(act.astype(BF16), wd_ref[...].astype(BF16), preferred_element_type=F32)

    if norm_output:
        @pl.when(f == pl.num_programs(1) - 1)
        def _():
            o_ref[...] = _rmsnorm_f32(o_ref[...], og_ref[...])


def _ffn(x, g, wg, wu, wd, lead, out_g, *, norm_output, tm, tf):
    m, d = x.shape
    f = wg.shape[-1]
    return pl.pallas_call(
        functools.partial(_ffn_kernel, norm_output=norm_output),
        grid=(m // tm, f // tf),
        in_specs=[
            pl.BlockSpec((tm, d), lambda i, j: (i, 0)),
            pl.BlockSpec((1, d), lambda i, j: (0, 0)),
            _stacked_spec(lead, (d, tf), lambda i, j: (0, j)),
            _stacked_spec(lead, (d, tf), lambda i, j: (0, j)),
            _stacked_spec(lead, (tf, d), lambda i, j: (j, 0)),
            pl.BlockSpec((1, d), lambda i, j: (0, 0)),
        ],
        out_specs=pl.BlockSpec((tm, d), lambda i, j: (i, 0)),
        out_shape=jax.ShapeDtypeStruct((m, d), F32),
        scratch_shapes=[pltpu.VMEM((tm, d), BF16)],
        compiler_params=_params("parallel", "arbitrary"),
        name="ffn",
    )(x, g.reshape(1, d), wg, wu, wd, out_g.reshape(1, d))


def _norm_matmul_kernel(x_ref, g_ref, w_ref, o_ref, h_ref):
    @pl.when(pl.program_id(1) == 0)
    def _():
        h_ref[...] = _rmsnorm_f32(x_ref[...], g_ref[...]).astype(BF16)

    o_ref[...] = jnp.dot(h_ref[...], w_ref[...].astype(BF16), preferred_element_type=F32).astype(o_ref.dtype)


def _norm_matmul(x, g, w, lead, *, tm, tn):
    m, d = x.shape
    n = w.shape[-1]
    return pl.pallas_call(
        _norm_matmul_kernel,
        grid=(m // tm, n // tn),
        in_specs=[
            pl.BlockSpec((tm, d), lambda i, j: (i, 0)),
            pl.BlockSpec((1, d), lambda i, j: (0, 0)),
            _stacked_spec(lead, (d, tn), lambda i, j: (0, j)),
        ],
        out_specs=pl.BlockSpec((tm, tn), lambda i, j: (i, j)),
        out_shape=jax.ShapeDtypeStruct((m, n), BF16),
        scratch_shapes=[pltpu.VMEM((tm, d), BF16)],
        compiler_params=_params("parallel", "arbitrary"),
        name="norm_matmul",
    )(x, g.reshape(1, d), w)


def _proj_residual_kernel(*refs, n_acts):
    a_refs, w_refs = refs[:n_acts], refs[n_acts:2 * n_acts]
    x_ref, o_ref = refs[2 * n_acts], refs[2 * n_acts + 1]
    acc = x_ref[...]
    for a_ref, w_ref in zip(a_refs, w_refs):
        acc += jnp.dot(a_ref[...], w_ref[...].astype(BF16), preferred_element_type=F32)
    o_ref[...] = acc


def _proj_residual(acts, w, lead, x, *, tm, tn):
    m, n = x.shape
    ka = acts[0].shape[1]
    assert all(a.shape[1] == ka for a in acts) and ka * len(acts) == w.shape[-2]
    in_specs = [pl.BlockSpec((tm, ka), lambda i, j: (i, 0)) for _ in acts]
    in_specs += [_stacked_spec(lead, (ka, tn), functools.partial(lambda i, j, p: (p, j), p=p))
                 for p in range(len(acts))]
    in_specs += [pl.BlockSpec((tm, tn), lambda i, j: (i, j))]
    return pl.pallas_call(
        functools.partial(_proj_residual_kernel, n_acts=len(acts)),
        grid=(m // tm, n // tn),
        in_specs=in_specs,
        out_specs=pl.BlockSpec((tm, tn), lambda i, j: (i, j)),
        out_shape=jax.ShapeDtypeStruct((m, n), F32),
        compiler_params=_params("parallel", "parallel"),
        name="proj_residual",
    )(*acts, *([w] * len(acts)), x)


def _t5_bucket(n):
    max_exact = REL_BUCKETS // 2
    nf = jnp.maximum(n, max_exact).astype(F32)
    large = max_exact + (jnp.log(nf / max_exact) / math.log(REL_MAX_DIST / max_exact)
                         * (REL_BUCKETS - max_exact)).astype(jnp.int32)
    large = jnp.minimum(large, REL_BUCKETS - 1)
    return jnp.where(n < max_exact, n, large)


def _bias_tile_kernel(rb_ref, o_ref):
    blk = ATTN_BLOCK
    h = pl.program_id(0)
    back = (lax.broadcasted_iota(jnp.int32, (blk, blk), 0)
            - lax.broadcasted_iota(jnp.int32, (blk, blk), 1))
    far = rb_ref[REL_BUCKETS - 1, h]
    for delta in range(2):
        bucket = _t5_bucket(jnp.maximum(delta * blk + back, 0))
        tile = jnp.zeros((blk, blk), F32)
        for b in range(REL_BUCKETS):
            tile = jnp.where(bucket == b, rb_ref[b, h] - far, tile)
        o_ref[0, delta] = tile


def _bias_tiles(rel_bias):
    blk = ATTN_BLOCK
    assert blk >= REL_MAX_DIST
    n_heads = rel_bias.shape[1]
    return pl.pallas_call(
        _bias_tile_kernel,
        grid=(n_heads,),
        in_specs=[pl.BlockSpec(memory_space=pltpu.SMEM)],
        out_specs=pl.BlockSpec((1, 2, blk, blk), lambda h: (h, 0, 0, 0)),
        out_shape=jax.ShapeDtypeStruct((n_heads, 2, blk, blk), F32),
        compiler_params=_params("parallel"),
        name="bias_tiles",
    )(rel_bias.astype(F32))


def _attn_kernel(lam_ref, q_ref, k_ref, v_ref, bias_ref, sg_ref, o_ref, *, lam_init):
    blk = ATTN_BLOCK
    d = DIFF_HEAD_DIM
    n_blocks = q_ref.shape[1] // blk
    future = (lax.broadcasted_iota(jnp.int32, (blk, blk), 1)
              > lax.broadcasted_iota(jnp.int32, (blk, blk), 0))
    lam_p = lam_ref[...]
    lam = (jnp.exp(jnp.sum(lam_p[0:1] * lam_p[1:2], axis=-1, keepdims=True))
           - jnp.exp(jnp.sum(lam_p[2:3] * lam_p[3:4], axis=-1, keepdims=True)) + lam_init)

    for c in range(n_blocks):
        n = (c + 1) * blk
        maps = []
        for half in range(2):
            cols = slice(half * d, (half + 1) * d)
            s = lax.dot_general(q_ref[0, c * blk:n, cols], k_ref[0, :n, cols],
                                (((1,), (1,)), ((), ())), preferred_element_type=F32) * (d ** -0.5)
            pieces = [jnp.where(future, NEG_INF, s[:, n - blk:] + bias_ref[0, 0])]
            if c >= 1:
                pieces.insert(0, s[:, n - 2 * blk:n - blk] + bias_ref[0, 1])
            if c >= 2:
                pieces.insert(0, s[:, :n - 2 * blk])
            s = jnp.concatenate(pieces, axis=1) if len(pieces) > 1 else pieces[0]
            p = jnp.exp(s - jnp.max(s, axis=-1, keepdims=True))
            inv_l = 1.0 / jnp.sum(p, axis=-1, keepdims=True)
            maps.append(jnp.dot(p.astype(BF16), v_ref[0, :n, :], preferred_element_type=F32) * inv_l)
        att = maps[0] - lam * maps[1]
        o_ref[0, c * blk:n, :] = (_rmsnorm_f32(att, sg_ref[...]) * (1.0 - lam_init)).astype(o_ref.dtype)


def _diff_attention(z, lam_params, bias_tiles, subln_g, *, lam_init, n_heads, col_q, col_k, col_v):
    b, s, _ = z.shape
    blk = ATTN_BLOCK
    dv = DIFF_V_DIM
    seq_spec = lambda col: pl.BlockSpec((1, s, dv), lambda bi, h: (bi, 0, col + h))
    return pl.pallas_call(
        functools.partial(_attn_kernel, lam_init=lam_init),
        grid=(b, n_heads),
        in_specs=[
            pl.BlockSpec((4, DIFF_HEAD_DIM), lambda bi, h: (0, 0)),
            seq_spec(col_q), seq_spec(col_k), seq_spec(col_v),
            pl.BlockSpec((1, 2, blk, blk), lambda bi, h: (h, 0, 0, 0)),
            pl.BlockSpec((1, dv), lambda bi, h: (0, 0)),
        ],
        out_specs=pl.BlockSpec((1, s, dv), lambda bi, h: (bi, 0, h)),
        out_shape=jax.ShapeDtypeStruct((b, s, n_heads * dv), BF16),
        compiler_params=_params("parallel", "parallel"),
        name="diff_attention",
    )(lam_params, z, z, z, bias_tiles, subln_g.reshape(1, dv))


def _pool_kernel(cur_ref, prev_ref, pw_ref, sc_ref, o_ref):
    blk = POOL_BLOCK
    gw = POOL_GROUP_WIDTH
    c = pl.program_id(1)
    row = lax.broadcasted_iota(jnp.int32, (blk, blk), 0)
    col = lax.broadcasted_iota(jnp.int32, (blk, blk), 1)
    back = row - col
    t = c * blk + lax.broadcasted_iota(jnp.int32, (blk, 1), 0)
    has_prev = (c > 0).astype(F32)
    for g, w in enumerate(POOL_WINDOWS):
        cols = slice(g * gw, (g + 1) * gw)
        in_cur = jnp.where((back >= 0) & (back < w), 1.0, 0.0).astype(BF16)
        in_prev = jnp.where(back + blk < w, 1.0, 0.0).astype(BF16)
        cur = cur_ref[0, :, cols]
        window_sum = (jnp.dot(in_cur, cur, preferred_element_type=F32)
                      + has_prev * jnp.dot(in_prev, prev_ref[0, :, cols], preferred_element_type=F32))
        cnt = jnp.minimum(t + 1, w).astype(F32)
        p = window_sum / cnt - cur.astype(F32)
        y = jnp.dot(p.astype(BF16), pw_ref[g], preferred_element_type=F32) * sc_ref[:, cols]
        o_ref[0, :, cols] = y.astype(o_ref.dtype)


def _pool_mixer(z, pool_w, lead, pool_scale):
    b, s, _ = z.shape
    blk = POOL_BLOCK
    pw = len(POOL_WINDOWS) * POOL_GROUP_WIDTH
    assert blk >= max(POOL_WINDOWS)
    return pl.pallas_call(
        _pool_kernel,
        grid=(b, s // blk),
        in_specs=[
            pl.BlockSpec((1, blk, pw), lambda bi, c: (bi, c, 0)),
            pl.BlockSpec((1, blk, pw), lambda bi, c: (bi, jnp.maximum(c - 1, 0), 0)),
            _stacked_spec(lead, pool_w.shape[-3:], lambda bi, c: (0, 0, 0)),
            pl.BlockSpec((1, pw), lambda bi, c: (0, 0)),
        ],
        out_specs=pl.BlockSpec((1, blk, pw), lambda bi, c: (bi, c, 0)),
        out_shape=jax.ShapeDtypeStruct((b, s, pw), BF16),
        compiler_params=_params("parallel", "parallel"),
        name="pool_mixer",
    )(z, z, pool_w, pool_scale.reshape(1, pw))


CONV_HALO = 8


def _norm_conv_in_kernel(x_ref, g_ref, wb_ref, wc_ref, wx_ref, cw_ref, o_ref, h_ref, tail_ref, *, tiles_per_seq):
    i, j = pl.program_id(0), pl.program_id(1)
    tm = x_ref.shape[0]

    @pl.when(j == 0)
    def _():
        h_ref[...] = _rmsnorm_f32(x_ref[...], g_ref[...]).astype(BF16)

    @pl.when(i % tiles_per_seq == 0)
    def _():
        tail_ref[j] = jnp.zeros(tail_ref.shape[1:], F32)

    h = h_ref[...]
    gate_b = jnp.dot(h, wb_ref[...].astype(BF16), preferred_element_type=F32)
    u = (jnp.dot(h, wc_ref[...].astype(BF16), preferred_element_type=F32)
         * jnp.dot(h, wx_ref[...].astype(BF16), preferred_element_type=F32))
    u_ext = jnp.concatenate([tail_ref[j], u], axis=0)
    tail_ref[j] = u[tm - CONV_HALO:, :]
    taps = cw_ref[...]
    y = taps[2:3] * u
    for back in (1, 2):
        y += taps[2 - back:3 - back] * pltpu.roll(u_ext, back, 0)[CONV_HALO:, :]
    o_ref[...] = (gate_b * y).astype(o_ref.dtype)


def _norm_conv_in(x, g, w, lead, conv_w, *, seq, tm, tc):
    m, d = x.shape
    nb = d // tc
    assert seq % tm == 0 and CONV_HALO >= conv_w.shape[0] - 1
    w_spec = lambda part: _stacked_spec(lead, (d, tc), lambda i, j: (0, part * nb + j))
    return pl.pallas_call(
        functools.partial(_norm_conv_in_kernel, tiles_per_seq=seq // tm),
        grid=(m // tm, nb),
        in_specs=[
            pl.BlockSpec((tm, d), lambda i, j: (i, 0)),
            pl.BlockSpec((1, d), lambda i, j: (0, 0)),
            w_spec(0), w_spec(1), w_spec(2),
            pl.BlockSpec((conv_w.shape[0], tc), lambda i, j: (0, j)),
        ],
        out_specs=pl.BlockSpec((tm, tc), lambda i, j: (i, j)),
        out_shape=jax.ShapeDtypeStruct((m, d), BF16),
        scratch_shapes=[pltpu.VMEM((tm, d), BF16), pltpu.VMEM((nb, CONV_HALO, tc), F32)],
        compiler_params=_params("arbitrary", "arbitrary"),
        name="norm_conv_in",
    )(x, g.reshape(1, d), w, w, w, conv_w)


def kernel(x, ffn_norm_g, mix_norm_g, final_norm_g, ffn_w_gate, ffn_w_up, ffn_w_down, ab_w_in, pool_w, pool_scale, lam_q1, lam_k1, lam_q2, lam_k2, subln_g, ab_w_out, rel_bias, conv_w_in, conv_w, conv_w_out):
    b, s, d = x.shape
    depth = ffn_norm_g.shape[0]
    pool_width = pool_scale.shape[1]
    n_heads = rel_bias.shape[1]
    dv_blocks = pool_width // DIFF_V_DIM

    w_gate, w_up, w_down = ffn_w_gate.astype(BF16), ffn_w_up.astype(BF16), ffn_w_down
    ab_in, ab_out, pool_wb = ab_w_in, ab_w_out, pool_w.astype(BF16)
    cv_in, cv_out = conv_w_in, conv_w_out
    bias_tiles = _bias_tiles(rel_bias)

    ffn = functools.partial(_ffn, tm=1024, tf=512)
    xm = x.reshape(b * s, d)
    for l in range(depth):
        xm = ffn(xm, ffn_norm_g[l, 0], w_gate, w_up, w_down, (l, 0), final_norm_g, norm_output=False)
        j = l // 2
        if l % 2 == 0:
            z = _norm_matmul(xm, mix_norm_g[l], ab_in, (j,), tm=1024, tn=1024).reshape(b, s, -1)
            lam_params = jnp.stack([lam_q1[j], lam_k1[j], lam_q2[j], lam_k2[j]]).astype(F32)
            att = _diff_attention(z, lam_params, bias_tiles, subln_g[j],
                                  lam_init=0.8 - 0.6 * math.exp(-0.3 * l), n_heads=n_heads,
                                  col_q=dv_blocks, col_k=dv_blocks + n_heads, col_v=dv_blocks + 2 * n_heads)
            pool = _pool_mixer(z, pool_wb, (j,), pool_scale[j])
            xm = _proj_residual([pool.reshape(b * s, -1), att.reshape(b * s, -1)], ab_out, (j,), xm,
                                tm=512, tn=d)
        else:
            gated = _norm_conv_in(xm, mix_norm_g[l], cv_in, (j,), conv_w[j], seq=s, tm=1024, tc=512)
            xm = _proj_residual([gated], cv_out, (j,), xm, tm=512, tn=d)
        xm = ffn(xm, ffn_norm_g[l, 1], w_gate, w_up, w_down, (l, 1), final_norm_g,
                 norm_output=(l == depth - 1))
    return xm.reshape(b, s, d)
```

```python
import functools
import math

import jax
import jax.numpy as jnp
from jax import lax
from jax.experimental import pallas as pl
from jax.experimental.pallas import tpu as pltpu

F32 = jnp.float32
BF16 = jnp.bfloat16

NORM_EPS = 1e-6
NEG_INF = -1e30
POOL_WINDOWS = (2, 4, 8, 16)
POOL_GROUP_WIDTH = 256
DIFF_HEAD_DIM = 128
DIFF_V_DIM = 2 * DIFF_HEAD_DIM
REL_BUCKETS = 32
REL_MAX_DIST = 128

V7X_VMEM_BYTES = 64 * 1024 * 1024
VMEM_LIMIT_BYTES = V7X_VMEM_BYTES - 2 * 1024 * 1024
LANES = 128

ATTN_BLOCK = 256
POOL_BLOCK = 256


def _params(*semantics):
    return pltpu.CompilerParams(dimension_semantics=semantics, vmem_limit_bytes=VMEM_LIMIT_BYTES)


def _rmsnorm_f32(x, g):
    return x * lax.rsqrt(jnp.mean(x * x, axis=-1, keepdims=True) + NORM_EPS) * g


def _stacked_spec(lead, block, index_map):
    return pl.BlockSpec((None,) * len(lead) + block, lambda *g: lead + index_map(*g))


def _ffn_kernel(x_ref, g_ref, wg_ref, wu_ref, wd_ref, og_ref, o_ref, h_ref, *, norm_output):
    f = pl.program_id(1)

    @pl.when(f == 0)
    def _():
        x = x_ref[...]
        h_ref[...] = _rmsnorm_f32(x, g_ref[...]).astype(BF16)
        o_ref[...] = x

    h = h_ref[...]
    gate = jnp.dot(h, wg_ref[...].astype(BF16), preferred_element_type=F32)
    up = jnp.dot(h, wu_ref[...].astype(BF16), preferred_element_type=F32)
    act = (0.5 * gate * jax.nn.sigmoid(gate)) * up
    o_ref[...] += jnp.dot(act.astype(BF16), wd_ref[...].astype(BF16), preferred_element_type=F32)

    if norm_output:
        @pl.when(f == pl.num_programs(1) - 1)
        def _():
            o_ref[...] = _rmsnorm_f32(o_ref[...], og_ref[...])


def _ffn(x, g, wg, wu, wd, lead, out_g, *, norm_output, tm, tf):
    m, d = x.shape
    f = wg.shape[-1]
    return pl.pallas_call(
        functools.partial(_ffn_kernel, norm_output=norm_output),
        grid=(m // tm, f // tf),
        in_specs=[
            pl.BlockSpec((tm, d), lambda i, j: (i, 0)),
            pl.BlockSpec((1, d), lambda i, j: (0, 0)),
            _stacked_spec(lead, (d, tf), lambda i, j: (0, j)),
            _stacked_spec(lead, (d, tf), lambda i, j: (0, j)),
            _stacked_spec(lead, (tf, d), lambda i, j: (j, 0)),
            pl.BlockSpec((1, d), lambda i, j: (0, 0)),
        ],
        out_specs=pl.BlockSpec((tm, d), lambda i, j: (i, 0)),
        out_shape=jax.ShapeDtypeStruct((m, d), F32),
        scratch_shapes=[pltpu.VMEM((tm, d), BF16)],
        compiler_params=_params("parallel", "arbitrary"),
        name="ffn",
    )(x, g.reshape(1, d), wg, wu, wd, out_g.reshape(1, d))


def _norm_matmul_kernel(x_ref, g_ref, w_ref, o_ref, h_ref):
    @pl.when(pl.program_id(1) == 0)
    def _():
        h_ref[...] = _rmsnorm_f32(x_ref[...], g_ref[...]).astype(BF16)

    o_ref[...] = jnp.dot(h_ref[...], w_ref[...].astype(BF16), preferred_element_type=F32).astype(o_ref.dtype)


def _norm_matmul(x, g, w, lead, *, tm, tn):
    m, d = x.shape
    n = w.shape[-1]
    return pl.pallas_call(
        _norm_matmul_kernel,
        grid=(m // tm, n // tn),
        in_specs=[
            pl.BlockSpec((tm, d), lambda i, j: (i, 0)),
            pl.BlockSpec((1, d), lambda i, j: (0, 0)),
            _stacked_spec(lead, (d, tn), lambda i, j: (0, j)),
        ],
        out_specs=pl.BlockSpec((tm, tn), lambda i, j: (i, j)),
        out_shape=jax.ShapeDtypeStruct((m, n), BF16),
        scratch_shapes=[pltpu.VMEM((tm, d), BF16)],
        compiler_params=_params("parallel", "arbitrary"),
        name="norm_matmul",
    )(x, g.reshape(1, d), w)


def _proj_residual_kernel(*refs, n_acts):
    a_refs, w_refs = refs[:n_acts], refs[n_acts:2 * n_acts]
    x_ref, o_ref = refs[2 * n_acts], refs[2 * n_acts + 1]
    acc = x_ref[...]
    for a_ref, w_ref in zip(a_refs, w_refs):
        acc += jnp.dot(a_ref[...], w_ref[...].astype(BF16), preferred_element_type=F32)
    o_ref[...] = acc


def _proj_residual(acts, w, lead, x, *, tm, tn):
    m, n = x.shape
    ka = acts[0].shape[1]
    assert all(a.shape[1] == ka for a in acts) and ka * len(acts) == w.shape[-2]
    in_specs = [pl.BlockSpec((tm, ka), lambda i, j: (i, 0)) for _ in acts]
    in_specs += [_stacked_spec(lead, (ka, tn), functools.partial(lambda i, j, p: (p, j), p=p))
                 for p in range(len(acts))]
    in_specs += [pl.BlockSpec((tm, tn), lambda i, j: (i, j))]
    return pl.pallas_call(
        functools.partial(_proj_residual_kernel, n_acts=len(acts)),
        grid=(m // tm, n // tn),
        in_specs=in_specs,
        out_specs=pl.BlockSpec((tm, tn), lambda i, j: (i, j)),
        out_shape=jax.ShapeDtypeStruct((m, n), F32),
        compiler_params=_params("parallel", "parallel"),
        name="proj_residual",
    )(*acts, *([w] * len(acts)), x)


def _t5_bucket(n):
    max_exact = REL_BUCKETS // 2
    nf = jnp.maximum(n, max_exact).astype(F32)
    large = max_exact + (jnp.log(nf / max_exact) / math.log(REL_MAX_DIST / max_exact)
                         * (REL_BUCKETS - max_exact)).astype(jnp.int32)
    large = jnp.minimum(large, REL_BUCKETS - 1)
    return jnp.where(n < max_exact, n, large)


def _bias_tile_kernel(rb_ref, o_ref):
    blk = ATTN_BLOCK
    h = pl.program_id(0)
    back = (lax.broadcasted_iota(jnp.int32, (blk, blk), 0)
            - lax.broadcasted_iota(jnp.int32, (blk, blk), 1))
    far = rb_ref[REL_BUCKETS - 1, h]
    for delta in range(2):
        bucket = _t5_bucket(jnp.maximum(delta * blk + back, 0))
        tile = jnp.zeros((blk, blk), F32)
        for b in range(REL_BUCKETS):
            tile = jnp.where(bucket == b, rb_ref[b, h] - far, tile)
        o_ref[0, delta] = tile


def _bias_tiles(rel_bias):
    blk = ATTN_BLOCK
    assert blk >= REL_MAX_DIST
    n_heads = rel_bias.shape[1]
    return pl.pallas_call(
        _bias_tile_kernel,
        grid=(n_heads,),
        in_specs=[pl.BlockSpec(memory_space=pltpu.SMEM)],
        out_specs=pl.BlockSpec((1, 2, blk, blk), lambda h: (h, 0, 0, 0)),
        out_shape=jax.ShapeDtypeStruct((n_heads, 2, blk, blk), F32),
        compiler_params=_params("parallel"),
        name="bias_tiles",
    )(rel_bias.astype(F32))


def _attn_kernel(lam_ref, q_ref, k_ref, v_ref, bias_ref, sg_ref, o_ref, *, lam_init):
    blk = ATTN_BLOCK
    d = DIFF_HEAD_DIM
    n_blocks = q_ref.shape[1] // blk
    future = (lax.broadcasted_iota(jnp.int32, (blk, blk), 1)
              > lax.broadcasted_iota(jnp.int32, (blk, blk), 0))
    lam_p = lam_ref[...]
    lam = (jnp.exp(jnp.sum(lam_p[0:1] * lam_p[1:2], axis=-1, keepdims=True))
           - jnp.exp(jnp.sum(lam_p[2:3] * lam_p[3:4], axis=-1, keepdims=True)) + lam_init)

    for c in range(n_blocks):
        n = (c + 1) * blk
        maps = []
        for half in range(2):
            cols = slice(half * d, (half + 1) * d)
            s = lax.dot_general(q_ref[0, c * blk:n, cols], k_ref[0, :n, cols],
                                (((1,), (1,)), ((), ())), preferred_element_type=F32) * (d ** -0.5)
            pieces = [jnp.where(future, NEG_INF, s[:, n - blk:] + bias_ref[0, 0])]
            if c >= 1:
                pieces.insert(0, s[:, n - 2 * blk:n - blk] + bias_ref[0, 1])
            if c >= 2:
                pieces.insert(0, s[:, :n - 2 * blk])
            s = jnp.concatenate(pieces, axis=1) if len(pieces) > 1 else pieces[0]
            p = jnp.exp(s - jnp.max(s, axis=-1, keepdims=True))
            inv_l = 1.0 / jnp.sum(p, axis=-1, keepdims=True)
            maps.append(jnp.dot(p.astype(BF16), v_ref[0, :n, :], preferred_element_type=F32) * inv_l)
        att = maps[0] - lam * maps[1]
        o_ref[0, c * blk:n, :] = (_rmsnorm_f32(att, sg_ref[...]) * (1.0 - lam_init)).astype(o_ref.dtype)


def _diff_attention(z, lam_params, bias_tiles, subln_g, *, lam_init, n_heads, col_q, col_k, col_v):
    b, s, _ = z.shape
    blk = ATTN_BLOCK
    dv = DIFF_V_DIM
    seq_spec = lambda col: pl.BlockSpec((1, s, dv), lambda bi, h: (bi, 0, col + h))
    return pl.pallas_call(
        functools.partial(_attn_kernel, lam_init=lam_init),
        grid=(b, n_heads),
        in_specs=[
            pl.BlockSpec((4, DIFF_HEAD_DIM), lambda bi, h: (0, 0)),
            seq_spec(col_q), seq_spec(col_k), seq_spec(col_v),
            pl.BlockSpec((1, 2, blk, blk), lambda bi, h: (h, 0, 0, 0)),
            pl.BlockSpec((1, dv), lambda bi, h: (0, 0)),
        ],
        out_specs=pl.BlockSpec((1, s, dv), lambda bi, h: (bi, 0, h)),
        out_shape=jax.ShapeDtypeStruct((b, s, n_heads * dv), BF16),
        compiler_params=_params("parallel", "parallel"),
        name="diff_attention",
    )(lam_params, z, z, z, bias_tiles, subln_g.reshape(1, dv))


def _pool_kernel(cur_ref, prev_ref, pw_ref, sc_ref, o_ref):
    blk = POOL_BLOCK
    gw = POOL_GROUP_WIDTH
    c = pl.program_id(1)
    row = lax.broadcasted_iota(jnp.int32, (blk, blk), 0)
    col = lax.broadcasted_iota(jnp.int32, (blk, blk), 1)
    back = row - col
    t = c * blk + lax.broadcasted_iota(jnp.int32, (blk, 1), 0)
    has_prev = (c > 0).astype(F32)
    for g, w in enumerate(POOL_WINDOWS):
        cols = slice(g * gw, (g + 1) * gw)
        in_cur = jnp.where((back >= 0) & (back < w), 1.0, 0.0).astype(BF16)
        in_prev = jnp.where(back + blk < w, 1.0, 0.0).astype(BF16)
        cur = cur_ref[0, :, cols]
        window_sum = (jnp.dot(in_cur, cur, preferred_element_type=F32)
                      + has_prev * jnp.dot(in_prev, prev_ref[0, :, cols], preferred_element_type=F32))
        cnt = jnp.minimum(t + 1, w).astype(F32)
        p = window_sum / cnt - cur.astype(F32)
        y = jnp.dot(p.astype(BF16), pw_ref[g], preferred_element_type=F32) * sc_ref[:, cols]
        o_ref[0, :, cols] = y.astype(o_ref.dtype)


def _pool_mixer(z, pool_w, lead, pool_scale):
    b, s, _ = z.shape
    blk = POOL_BLOCK
    pw = len(POOL_WINDOWS) * POOL_GROUP_WIDTH
    assert blk >= max(POOL_WINDOWS)
    return pl.pallas_call(
        _pool_kernel,
        grid=(b, s // blk),
        in_specs=[
            pl.BlockSpec((1, blk, pw), lambda bi, c: (bi, c, 0)),
            pl.BlockSpec((1, blk, pw), lambda bi, c: (bi, jnp.maximum(c - 1, 0), 0)),
            _stacked_spec(lead, pool_w.shape[-3:], lambda bi, c: (0, 0, 0)),
            pl.BlockSpec((1, pw), lambda bi, c: (0, 0)),
        ],
        out_specs=pl.BlockSpec((1, blk, pw), lambda bi, c: (bi, c, 0)),
        out_shape=jax.ShapeDtypeStruct((b, s, pw), BF16),
        compiler_params=_params("parallel", "parallel"),
        name="pool_mixer",
    )(z, z, pool_w, pool_scale.reshape(1, pw))


CONV_HALO = 8


def _norm_conv_in_kernel(x_ref, g_ref, wb_ref, wc_ref, wx_ref, cw_ref, o_ref, h_ref, tail_ref, *, tiles_per_seq):
    i, j = pl.program_id(0), pl.program_id(1)
    tm = x_ref.shape[0]

    @pl.when(j == 0)
    def _():
        h_ref[...] = _rmsnorm_f32(x_ref[...], g_ref[...]).astype(BF16)

    @pl.when(i % tiles_per_seq == 0)
    def _():
        tail_ref[j] = jnp.zeros(tail_ref.shape[1:], F32)

    h = h_ref[...]
    gate_b = jnp.dot(h, wb_ref[...].astype(BF16), preferred_element_type=F32)
    u = (jnp.dot(h, wc_ref[...].astype(BF16), preferred_element_type=F32)
         * jnp.dot(h, wx_ref[...].astype(BF16), preferred_element_type=F32))
    u_ext = jnp.concatenate([tail_ref[j], u], axis=0)
    tail_ref[j] = u[tm - CONV_HALO:, :]
    taps = cw_ref[...]
    y = taps[2:3] * u
    for back in (1, 2):
        y += taps[2 - back:3 - back] * pltpu.roll(u_ext, back, 0)[CONV_HALO:, :]
    o_ref[...] = (gate_b * y).astype(o_ref.dtype)


def _norm_conv_in(x, g, w, lead, conv_w, *, seq, tm, tc):
    m, d = x.shape
    nb = d // tc
    assert seq % tm == 0 and CONV_HALO >= conv_w.shape[0] - 1
    w_spec = lambda part: _stacked_spec(lead, (d, tc), lambda i, j: (0, part * nb + j))
    return pl.pallas_call(
        functools.partial(_norm_conv_in_kernel, tiles_per_seq=seq // tm),
        grid=(m // tm, nb),
        in_specs=[
            pl.BlockSpec((tm, d), lambda i, j: (i, 0)),
            pl.BlockSpec((1, d), lambda i, j: (0, 0)),
            w_spec(0), w_spec(1), w_spec(2),
            pl.BlockSpec((conv_w.shape[0], tc), lambda i, j: (0, j)),
        ],
        out_specs=pl.BlockSpec((tm, tc), lambda i, j: (i, j)),
        out_shape=jax.ShapeDtypeStruct((m, d), BF16),
        scratch_shapes=[pltpu.VMEM((tm, d), BF16), pltpu.VMEM((nb, CONV_HALO, tc), F32)],
        compiler_params=_params("arbitrary", "arbitrary"),
        name="norm_conv_in",
    )(x, g.reshape(1, d), w, w, w, conv_w)


def kernel(x, ffn_norm_g, mix_norm_g, final_norm_g, ffn_w_gate, ffn_w_up, ffn_w_down, ab_w_in, pool_w, pool_scale, lam_q1, lam_k1, lam_q2, lam_k2, subln_g, ab_w_out, rel_bias, conv_w_in, conv_w, conv_w_out):
    b, s, d = x.shape
    depth = ffn_norm_g.shape[0]
    pool_width = pool_scale.shape[1]
    n_heads = rel_bias.shape[1]
    dv_blocks = pool_width // DIFF_V_DIM

    w_gate, w_up, w_down = ffn_w_gate.astype(BF16), ffn_w_up.astype(BF16), ffn_w_down
    ab_in, ab_out, pool_wb = ab_w_in.astype(BF16), ab_w_out, pool_w.astype(BF16)
    cv_in, cv_out = conv_w_in.astype(BF16), conv_w_out
    bias_tiles = _bias_tiles(rel_bias)

    ffn = functools.partial(_ffn, tm=1024, tf=512)
    xm = x.reshape(b * s, d)
    for l in range(depth):
        xm = ffn(xm, ffn_norm_g[l, 0], w_gate, w_up, w_down, (l, 0), final_norm_g, norm_output=False)
        j = l // 2
        if l % 2 == 0:
            z = _norm_matmul(xm, mix_norm_g[l], ab_in, (j,), tm=1024, tn=1024).reshape(b, s, -1)
            lam_params = jnp.stack([lam_q1[j], lam_k1[j], lam_q2[j], lam_k2[j]]).astype(F32)
            att = _diff_attention(z, lam_params, bias_tiles, subln_g[j],
                                  lam_init=0.8 - 0.6 * math.exp(-0.3 * l), n_heads=n_heads,
                                  col_q=dv_blocks, col_k=dv_blocks + n_heads, col_v=dv_blocks + 2 * n_heads)
            pool = _pool_mixer(z, pool_wb, (j,), pool_scale[j])
            xm = _proj_residual([pool.reshape(b * s, -1), att.reshape(b * s, -1)], ab_out, (j,), xm,
                                tm=512, tn=d)
        else:
            gated = _norm_conv_in(xm, mix_norm_g[l], cv_in, (j,), conv_w[j], seq=s, tm=1024, tc=512)
            xm = _proj_residual([gated], cv_out, (j,), xm, tm=512, tn=d)
        xm = ffn(xm, ffn_norm_g[l, 1], w_gate, w_up, w_down, (l, 1), final_norm_g,
                 norm_output=(l == depth - 1))
    return xm.reshape(b, s, d)
```
